```python
import jax
import jax.numpy as jnp
from jax import lax
import numpy as np

D_MODEL = 1024
BATCH = 8
SEQ = 4096
DEPTH = 2
DEC_BATCH = 4
DEC_SEQ = 8192
PAST_LEN = 128

GRID_W = 64
HEAD_DIM = 64
NA_HEADS = 8
NA_WIN_H = 8
NA_WIN_W = 16
MLA_HEADS = 8
MLA_Q_RANK = 256
MLA_KV_RANK = 128
MLA_NOPE = 64
MLA_ROPE = 32
MLA_V = 64
MLA_BLOCK = 128
DIL_PAIRS = ((128, 1), (512, 4), (2048, 16))
DIL_HEADS = 8
DIL_BLOCK = 128
N_EXPERTS = 16
EC_CAPACITY = 2
EXPERT_FF = 2816
ROPE_THETA = 10000.0
LN_EPS = 1e-5
RMS_EPS = 1e-6
DEEPNORM_ALPHA = (2 * DEPTH) ** 0.25
DEEPNORM_BETA = (8 * DEPTH) ** -0.25
N_AB_LAYERS = (DEPTH + 1) // 2
N_C_LAYERS = DEPTH // 2
NA_WIDTH = NA_HEADS * HEAD_DIM
MLA_Q_COLS = MLA_HEADS * (MLA_NOPE + MLA_ROPE)
MLA_KV_COLS = MLA_HEADS * (MLA_NOPE + MLA_V)
AB_IN_COLS = 3 * NA_WIDTH + MLA_Q_RANK + MLA_KV_RANK + MLA_ROPE
AB_OUT_COLS = NA_WIDTH + MLA_HEADS * MLA_V
DIL_WIDTH = DIL_HEADS * HEAD_DIM
DIL_IN_COLS = 3 * len(DIL_PAIRS) * DIL_WIDTH

kernel_name = 'hybrid_na_mla_dilated_ec_encoder'


def layer_norm(x, g, b):
    xf = x.astype(jnp.float32)
    mu = jnp.mean(xf, -1, keepdims=True)
    var = jnp.mean(jnp.square(xf - mu), -1, keepdims=True)
    return ((xf - mu) * lax.rsqrt(var + LN_EPS) * g + b).astype(x.dtype)


def rms_norm(x, g):
    xf = x.astype(jnp.float32)
    return (xf * lax.rsqrt(jnp.mean(xf * xf, -1, keepdims=True) + RMS_EPS) * g).astype(x.dtype)


def rope_tables(n, dim):
    inv = 1.0 / (ROPE_THETA ** (jnp.arange(0, dim, 2, dtype=jnp.float32) / dim))
    ang = jnp.arange(n, dtype=jnp.float32)[:, None] * inv[None, :]
    return jnp.cos(ang), jnp.sin(ang)


def apply_rope(x, cos, sin):
    x1, x2 = jnp.split(x.astype(jnp.float32), 2, axis=-1)
    c = cos[:, None, :]
    s = sin[:, None, :]
    return jnp.concatenate([x1 * c - x2 * s, x1 * s + x2 * c], axis=-1).astype(x.dtype)


def ada_modulation(c, w, b):
    mod = jax.nn.silu(c) @ w + b
    shift, scale, gate = jnp.split(mod[:, None, :], 3, axis=-1)
    return shift, scale, gate


def neighbourhood_attention(q, k, v, rpb):
    B, S, H, dh = q.shape
    rows = S // GRID_W
    kh = min(NA_WIN_H, rows)
    kw = NA_WIN_W
    qg = q.reshape(B, rows, GRID_W, H, dh)
    kg = k.reshape(B, rows, GRID_W, H, dh)
    vg = v.reshape(B, rows, GRID_W, H, dh)
    cols = jnp.arange(GRID_W)
    c0 = jnp.clip(cols - kw // 2, 0, GRID_W - kw)
    col_idx = c0[:, None] + jnp.arange(kw)[None, :]
    col_off = col_idx - cols[:, None] + (NA_WIN_W - 1)
    scale = dh ** -0.5

    def row_step(args):
        r, q_row = args
        r0 = jnp.clip(r - kh // 2, 0, rows - kh)
        k_rows = lax.dynamic_slice_in_dim(kg, r0, kh, axis=1)
        v_rows = lax.dynamic_slice_in_dim(vg, r0, kh, axis=1)
        k_win = k_rows[:, :, col_idx]
        v_win = v_rows[:, :, col_idx]
        row_off = r0 + jnp.arange(kh) - r + (NA_WIN_H - 1)
        bias = rpb[:, row_off[:, None, None], col_off[None, :, :]]
        s = jnp.einsum('bqhd,brqjhd->bhqrj', q_row, k_win).astype(jnp.float32) * scale
        s = s + jnp.transpose(bias, (0, 2, 1, 3))[None].astype(jnp.float32)
        p = jax.nn.softmax(s.reshape(B, H, GRID_W, kh * kw), axis=-1)
        p = p.reshape(B, H, GRID_W, kh, kw).astype(v.dtype)
        return jnp.einsum('bhqrj,brqjhd->bqhd', p, v_win)

    out = lax.map(row_step, (jnp.arange(rows), jnp.moveaxis(qg, 1, 0)))
    return jnp.moveaxis(out, 0, 1).reshape(B, S, H, dh)


def mla_attention(q_lat, kv_lat, k_rope, q_norm, w_q_up, kv_norm, w_kv_up, cos_r, sin_r):
    B, S, _ = q_lat.shape
    q = (rms_norm(q_lat, q_norm) @ w_q_up).reshape(B, S, MLA_HEADS, MLA_NOPE + MLA_ROPE)
    q_nope = q[..., :MLA_NOPE]
    q_pe = apply_rope(q[..., MLA_NOPE:], cos_r, sin_r)
    kv = (rms_norm(kv_lat, kv_norm) @ w_kv_up).reshape(B, S, MLA_HEADS, MLA_NOPE + MLA_V)
    k_nope = kv[..., :MLA_NOPE]
    v = kv[..., MLA_NOPE:]
    k_pe = apply_rope(k_rope[:, :, None, :], cos_r, sin_r)[:, :, 0]
    scale = (MLA_NOPE + MLA_ROPE) ** -0.5
    nblk = S // MLA_BLOCK
    qn_blocks = q_nope.reshape(B, nblk, MLA_BLOCK, MLA_HEADS, MLA_NOPE).swapaxes(0, 1)
    qp_blocks = q_pe.reshape(B, nblk, MLA_BLOCK, MLA_HEADS, MLA_ROPE).swapaxes(0, 1)

    def block(args):
        qn, qp = args
        s = (jnp.einsum('bqhd,bkhd->bhqk', qn, k_nope).astype(jnp.float32)
             + jnp.einsum('bqhd,bkd->bhqk', qp, k_pe).astype(jnp.float32)) * scale
        p = jax.nn.softmax(s, axis=-1).astype(v.dtype)
        return jnp.einsum('bhqk,bkhd->bqhd', p, v)

    out = lax.map(block, (qn_blocks, qp_blocks))
    return out.swapaxes(0, 1).reshape(B, S, MLA_HEADS * MLA_V)


def na_mla_mixer(h, w_in, rpb, q_norm, w_q_up, kv_norm, w_kv_up, w_out, cos_r, sin_r):
    B, S, _ = h.shape
    proj = h @ w_in
    s1 = 3 * NA_WIDTH
    s2 = s1 + MLA_Q_RANK
    s3 = s2 + MLA_KV_RANK
    na_qkv = proj[..., :s1].reshape(B, S, 3, NA_HEADS, HEAD_DIM)
    na_out = neighbourhood_attention(na_qkv[:, :, 0], na_qkv[:, :, 1], na_qkv[:, :, 2], rpb)
    mla_out = mla_attention(proj[..., s1:s2], proj[..., s2:s3], proj[..., s3:],
                            q_norm, w_q_up, kv_norm, w_kv_up, cos_r, sin_r)
    return jnp.concatenate([na_out.reshape(B, S, NA_WIDTH), mla_out], axis=-1) @ w_out


def dilated_group(q, k, v, window, dilation):
    B, S, H, dh = q.shape
    half = (window // 2) // dilation
    offs = dilation * jnp.arange(-half, half + 1)
    nblk = S // DIL_BLOCK
    q_blocks = q.reshape(B, nblk, DIL_BLOCK, H, dh).swapaxes(0, 1)
    scale = dh ** -0.5

    def block(args):
        i, qb = args
        pos = i * DIL_BLOCK + jnp.arange(DIL_BLOCK)
        idx = pos[:, None] + offs[None, :]
        valid = (idx >= 0) & (idx < S)
        idx = jnp.clip(idx, 0, S - 1)
        kb = k[:, idx]
        vb = v[:, idx]
        s = jnp.einsum('bqhd,bqnhd->bhqn', qb, kb).astype(jnp.float32) * scale
        s = jnp.where(valid[None, None], s, -jnp.inf)
        lse = jax.nn.logsumexp(s, axis=-1, keepdims=True)
        p = jnp.exp(s - lse).astype(v.dtype)
        o = jnp.einsum('bhqn,bqnhd->bqhd', p, vb)
        return o, jnp.transpose(lse[..., 0], (0, 2, 1))

    o, lse = lax.map(block, (jnp.arange(nblk), q_blocks))
    return o.swapaxes(0, 1).reshape(B, S, H, dh), lse.swapaxes(0, 1).reshape(B, S, H)


def dilated_mixer(h, w_in, w_out, cos_h, sin_h):
    B, S, _ = h.shape
    qkv = (h @ w_in).reshape(B, S, len(DIL_PAIRS), 3, DIL_HEADS, HEAD_DIM)
    outs = []
    lses = []
    for g, (window, dilation) in enumerate(DIL_PAIRS):
        q = apply_rope(qkv[:, :, g, 0], cos_h, sin_h)
        k = apply_rope(qkv[:, :, g, 1], cos_h, sin_h)
        o, lse = dilated_group(q, k, qkv[:, :, g, 2], window, dilation)
        outs.append(o)
        lses.append(lse)
    wts = jax.nn.softmax(jnp.stack(lses, axis=0), axis=0).astype(h.dtype)
    out = jnp.einsum('gbsh,gbshd->bshd', wts, jnp.stack(outs, axis=0))
    return out.reshape(B, S, DIL_WIDTH) @ w_out


def expert_choice_ffn(h, w_router, w_gate, w_up, w_down):
    B, S, D = h.shape
    n_tok = B * S
    cap = EC_CAPACITY * n_tok // N_EXPERTS
    xt = h.reshape(n_tok, D)
    aff = jax.nn.softmax((xt @ w_router).astype(jnp.float32), axis=-1)
    gates, idx = lax.top_k(aff.T, cap)

    def expert(args):
        wg, wu, wd, ix = args
        xe = xt[ix]
        return (jax.nn.silu(xe @ wg) * (xe @ wu)) @ wd

    ye = lax.map(expert, (w_gate, w_up, w_down, idx))
    contrib = (gates[..., None].astype(ye.dtype) * ye).reshape(-1, D)
    y = jnp.zeros_like(xt).at[idx.reshape(-1)].add(contrib)
    return y.reshape(B, S, D)


def trunk(x, c, ab_w_in, na_rpb, mla_q_norm, mla_w_q_up, mla_kv_norm, mla_w_kv_up, ab_w_out,
          dil_w_in, dil_w_out, mix_ada_w, mix_ada_b, mix_ln_g, mix_ln_b,
          moe_ada_w, moe_ada_b, moe_ln_g, moe_ln_b, moe_w_router, moe_w_gate, moe_w_up, moe_w_down):
    S = x.shape[1]
    cos_h, sin_h = rope_tables(S, HEAD_DIM)
    cos_r, sin_r = rope_tables(S, MLA_ROPE)
    for layer in range(DEPTH):
        j = layer // 2
        shift, scale, gate = ada_modulation(c, mix_ada_w[layer], mix_ada_b[layer])
        h = x * (1 + scale) + shift
        if layer % 2 == 0:
            out = na_mla_mixer(h, ab_w_in[j], na_rpb[j], mla_q_norm[j], mla_w_q_up[j],
                               mla_kv_norm[j], mla_w_kv_up[j], ab_w_out[j], cos_r, sin_r)
        else:
            out = dilated_mixer(h, dil_w_in[j], dil_w_out[j], cos_h, sin_h)
        x = layer_norm(DEEPNORM_ALPHA * x + gate * out, mix_ln_g[layer], mix_ln_b[layer])
        shift, scale, gate = ada_modulation(c, moe_ada_w[layer], moe_ada_b[layer])
        h = x * (1 + scale) + shift
        out = expert_choice_ffn(h, moe_w_router[layer], moe_w_gate[layer], moe_w_up[layer],
                                moe_w_down[layer])
        x = layer_norm(DEEPNORM_ALPHA * x + gate * out, moe_ln_g[layer], moe_ln_b[layer])
    return x


def setup_inputs(seed: int = 0) -> dict:
    key = jax.random.key(seed)
    ks = jax.random.split(key, 25)
    D = D_MODEL
    L = DEPTH
    La = N_AB_LAYERS
    Lc = N_C_LAYERS

    def nrm(k, shape, std):
        return jax.random.normal(k, shape, jnp.float32) * std

    return {
        'x_prompt': nrm(ks[0], (BATCH, SEQ, D), 1.0),
        'x_sample': nrm(ks[1], (DEC_BATCH, DEC_SEQ, D), 1.0),
        'c_prompt': nrm(ks[2], (BATCH, D), 1.0),
        'c_sample': nrm(ks[3], (DEC_BATCH, D), 1.0),
        'ab_w_in': nrm(ks[4], (La, D, AB_IN_COLS), D ** -0.5),
        'na_rpb': nrm(ks[5], (La, NA_HEADS, 2 * NA_WIN_H - 1, 2 * NA_WIN_W - 1), 0.1),
        'mla_q_norm': 1.0 + nrm(ks[6], (La, MLA_Q_RANK), 0.01),
        'mla_w_q_up': nrm(ks[7], (La, MLA_Q_RANK, MLA_Q_COLS), MLA_Q_RANK ** -0.5),
        'mla_kv_norm': 1.0 + nrm(ks[8], (La, MLA_KV_RANK), 0.01),
        'mla_w_kv_up': nrm(ks[9], (La, MLA_KV_RANK, MLA_KV_COLS), MLA_KV_RANK ** -0.5),
        'ab_w_out': nrm(ks[10], (La, AB_OUT_COLS, D), DEEPNORM_BETA * AB_OUT_COLS ** -0.5),
        'dil_w_in': nrm(ks[11], (Lc, D, DIL_IN_COLS), D ** -0.5),
        'dil_w_out': nrm(ks[12], (Lc, DIL_WIDTH, D), DEEPNORM_BETA * DIL_WIDTH ** -0.5),
        'mix_ada_w': nrm(ks[13], (L, D, 3 * D), 0.5 * D ** -0.5),
        'mix_ada_b': nrm(ks[14], (L, 3 * D), 0.01),
        'mix_ln_g': 1.0 + nrm(ks[15], (L, D), 0.01),
        'mix_ln_b': nrm(ks[16], (L, D), 0.01),
        'moe_ada_w': nrm(ks[17], (L, D, 3 * D), 0.5 * D ** -0.5),
        'moe_ada_b': nrm(ks[18], (L, 3 * D), 0.01),
        'moe_ln_g': 1.0 + nrm(ks[19], (L, D), 0.01),
        'moe_ln_b': nrm(ks[20], (L, D), 0.01),
        'moe_w_router': nrm(ks[21], (L, D, N_EXPERTS), D ** -0.5),
        'moe_w_gate': nrm(ks[22], (L, N_EXPERTS, D, EXPERT_FF), D ** -0.5),
        'moe_w_up': nrm(ks[23], (L, N_EXPERTS, D, EXPERT_FF), D ** -0.5),
        'moe_w_down': nrm(ks[24], (L, N_EXPERTS, EXPERT_FF, D), DEEPNORM_BETA * EXPERT_FF ** -0.5),
    }


def reference(x_prompt, x_sample, c_prompt, c_sample, ab_w_in, na_rpb, mla_q_norm, mla_w_q_up,
              mla_kv_norm, mla_w_kv_up, ab_w_out, dil_w_in, dil_w_out, mix_ada_w, mix_ada_b,
              mix_ln_g, mix_ln_b, moe_ada_w, moe_ada_b, moe_ln_g, moe_ln_b, moe_w_router,
              moe_w_gate, moe_w_up, moe_w_down):
    y_prompt = trunk(x_prompt, c_prompt, ab_w_in, na_rpb, mla_q_norm, mla_w_q_up, mla_kv_norm,
                     mla_w_kv_up, ab_w_out, dil_w_in, dil_w_out, mix_ada_w, mix_ada_b, mix_ln_g,
                     mix_ln_b, moe_ada_w, moe_ada_b, moe_ln_g, moe_ln_b, moe_w_router, moe_w_gate,
                     moe_w_up, moe_w_down)
    y_sample = trunk(x_sample, c_sample, ab_w_in, na_rpb, mla_q_norm, mla_w_q_up, mla_kv_norm,
                     mla_w_kv_up, ab_w_out, dil_w_in, dil_w_out, mix_ada_w, mix_ada_b, mix_ln_g,
                     mix_ln_b, moe_ada_w, moe_ada_b, moe_ln_g, moe_ln_b, moe_w_router, moe_w_gate,
                     moe_w_up, moe_w_down)
    return (y_prompt, y_sample)
```

```python
import functools

import numpy as np
import jax
import jax.numpy as jnp
from jax import lax
from jax.experimental import pallas as pl
from jax.experimental.pallas import tpu as pltpu

GRID_W = 64
HEAD_DIM = 64
NA_HEADS = 8
NA_WIN_H = 8
NA_WIN_W = 16
MLA_HEADS = 8
MLA_Q_RANK = 256
MLA_KV_RANK = 128
MLA_NOPE = 64
MLA_ROPE = 32
MLA_V = 64
DIL_PAIRS = ((128, 1), (512, 4), (2048, 16))
DIL_HEADS = 8
N_EXPERTS = 16
EC_CAPACITY = 2
ROPE_THETA = 10000.0
LN_EPS = 1e-5
RMS_EPS = 1e-6
DEPTH = 2
DEEPNORM_ALPHA = (2 * DEPTH) ** 0.25
NA_WIDTH = NA_HEADS * HEAD_DIM
DIL_WIDTH = DIL_HEADS * HEAD_DIM

LANES = 128
BF16_SUBLANES = 16
MASK_VALUE = -1e30
VMEM_LIMIT = 56 * 1024 * 1024

_F32 = jnp.float32
_BF16 = jnp.bfloat16


def _cparams(sem, vmem=None):
    return pltpu.CompilerParams(dimension_semantics=sem, vmem_limit_bytes=vmem)


def _dotf(a, b):
    return jnp.dot(a, b, preferred_element_type=_F32)


def _dot_nt(a, b):
    return lax.dot_general(a, b, (((1,), (1,)), ((), ())), preferred_element_type=_F32)


def _split2(x):
    hi = x.astype(_BF16)
    lo = (x - hi.astype(_F32)).astype(_BF16)
    return hi, lo


def _split3(x):
    hi = x.astype(_BF16)
    r = x - hi.astype(_F32)
    mid = r.astype(_BF16)
    lo = (r - mid.astype(_F32)).astype(_BF16)
    return hi, mid, lo


def _silu(x):
    return x / (1.0 + jnp.exp(-x))


def _rope_rotate(x, half, first_half):
    n = x.shape[-1]
    fwd = pltpu.roll(x, n - half, 1)
    bwd = pltpu.roll(x, half, 1)
    return jnp.where(first_half, fwd, bwd)


def _ada_kernel(c_ref, w_ref, b_ref, o_ref):
    a = _silu(c_ref[...])
    ah, al = _split2(a)
    wh, wl = _split2(w_ref[...])
    o_ref[...] = _dotf(ah, wh) + (_dotf(ah, wl) + _dotf(al, wh)) + b_ref[...]


def _ada_mod(c, w, b):
    L, D, N3 = w.shape
    Bp = c.shape[0]
    tn = min(512, N3)
    return pl.pallas_call(
        _ada_kernel,
        out_shape=jax.ShapeDtypeStruct((L, Bp, N3), _F32),
        grid=(L, N3 // tn),
        in_specs=[
            pl.BlockSpec((Bp, D), lambda l, j: (0, 0)),
            pl.BlockSpec((None, D, tn), lambda l, j: (l, 0, j)),
            pl.BlockSpec((None, 1, tn), lambda l, j: (l, 0, j)),
        ],
        out_specs=pl.BlockSpec((None, Bp, tn), lambda l, j: (l, 0, j)),
        compiler_params=_cparams(("arbitrary", "arbitrary")),
        name="ada_mod",
    )(c, w, b.reshape(L, 1, N3))


def _inproj_kernel(x_ref, shift_ref, scale_ref, w_ref, cos_ref, sin_ref, o_ref, *, rope_cols, half):
    h = x_ref[...] * (1.0 + scale_ref[...]) + shift_ref[...]
    acc = _dotf(h.astype(_BF16), w_ref[...])
    n = acc.shape[1]
    if rope_cols:
        cos = cos_ref[...]
        sin = sin_ref[...]
        lane = lax.broadcasted_iota(jnp.int32, cos.shape, 1)
        first = (lane % (2 * half)) < half
        for j in range(rope_cols // LANES):
            blk = acc[:, j * LANES:(j + 1) * LANES]
            blk = blk * cos + _rope_rotate(blk, half, first) * sin
            o_ref[:, j * LANES:(j + 1) * LANES] = blk.astype(o_ref.dtype)
        o_ref[:, rope_cols:] = acc[:, rope_cols:].astype(o_ref.dtype)
    else:
        o_ref[...] = acc.astype(o_ref.dtype)


def _inproj(x, shift, scale, w, cos, sin, *, dil, rope_cols, half, tm):
    B, S, D = x.shape
    N = w.shape[1]
    L = S // dil
    tm = min(tm, L)
    xv = x.reshape(B, L, dil * D)
    cv = cos.reshape(L, dil * LANES)
    sv = sin.reshape(L, dil * LANES)
    kern = functools.partial(_inproj_kernel, rope_cols=rope_cols, half=half)
    return pl.pallas_call(
        kern,
        out_shape=jax.ShapeDtypeStruct((B, dil, L, N), _BF16),
        grid=(B, dil, L // tm),
        in_specs=[
            pl.BlockSpec((None, tm, D), lambda b, r, j: (b, j, r)),
            pl.BlockSpec((None, 1, D), lambda b, r, j: (b, 0, 0)),
            pl.BlockSpec((None, 1, D), lambda b, r, j: (b, 0, 0)),
            pl.BlockSpec((D, N), lambda b, r, j: (0, 0)),
            pl.BlockSpec((tm, LANES), lambda b, r, j: (j, r)),
            pl.BlockSpec((tm, LANES), lambda b, r, j: (j, r)),
        ],
        out_specs=pl.BlockSpec((None, None, tm, N), lambda b, r, j: (b, r, j, 0)),
        compiler_params=_cparams(("arbitrary",) * 3, VMEM_LIMIT),
        name="inproj",
    )(xv, shift, scale, w, cv, sv)


def _na_kernel(q_ref, k_ref, v_ref, bias_ref, o_ref, *, rows):
    nk = NA_WIN_H * GRID_W
    lane = lax.broadcasted_iota(jnp.int32, (GRID_W, LANES), 1)
    low = lane < HEAD_DIM

    def row_step(r, carry):
        r0 = jnp.clip(r - NA_WIN_H // 2, 0, rows - NA_WIN_H)
        case = r - r0
        q = q_ref[pl.ds(pl.multiple_of(r * GRID_W, GRID_W), GRID_W), :]
        kw = k_ref[pl.ds(pl.multiple_of(r0 * GRID_W, GRID_W), nk), :]
        vw = v_ref[pl.ds(pl.multiple_of(r0 * GRID_W, GRID_W), nk), :]
        outs = []
        for hh in range(2):
            sel = low if hh == 0 else jnp.logical_not(low)
            qh = jnp.where(sel, q, jnp.zeros_like(q)) * jnp.asarray(HEAD_DIM ** -0.5, q.dtype)
            s = _dot_nt(qh, kw) + bias_ref[hh, case]
            m = jnp.max(s, axis=1, keepdims=True)
            p = jnp.exp(s - m)
            l = jnp.sum(p, axis=1, keepdims=True)
            outs.append(_dotf(p.astype(_BF16), vw) / l)
        o = jnp.where(low, outs[0], outs[1])
        o_ref[pl.ds(pl.multiple_of(r * GRID_W, GRID_W), GRID_W), :] = o.astype(o_ref.dtype)
        return carry

    lax.fori_loop(0, rows, row_step, 0)


def _na_bias_table(rpb):
    kh, kw = NA_WIN_H, NA_WIN_W
    cols = np.arange(GRID_W)
    c0 = np.clip(cols - kw // 2, 0, GRID_W - kw)
    case = np.arange(kh)[:, None, None, None]
    j = np.arange(kh)[None, None, :, None]
    c = cols[None, :, None, None]
    cp = cols[None, None, None, :]
    row_off = np.broadcast_to(j - case + (kh - 1), (kh, GRID_W, kh, GRID_W))
    col_off = np.broadcast_to(cp - c + (kw - 1), (kh, GRID_W, kh, GRID_W))
    valid = np.broadcast_to((cp >= c0[None, :, None, None]) & (cp < c0[None, :, None, None] + kw),
                            (kh, GRID_W, kh, GRID_W))
    col_off = np.clip(col_off, 0, 2 * kw - 2)
    tab = rpb[:, row_off, col_off]
    tab = jnp.where(valid[None], tab, MASK_VALUE)
    return tab.reshape(rpb.shape[0], kh, GRID_W, kh * GRID_W).astype(_F32)


def _na_attention(proj, bias_tab):
    B, S, _ = proj.shape
    rows = S // GRID_W
    assert rows >= NA_WIN_H
    npair = NA_HEADS // 2
    kern = functools.partial(_na_kernel, rows=rows)
    return pl.pallas_call(
        kern,
        out_shape=jax.ShapeDtypeStruct((B, S, NA_WIDTH), _BF16),
        grid=(B, npair),
        in_specs=[
            pl.BlockSpec((None, S, LANES), lambda b, p: (b, 0, p)),
            pl.BlockSpec((None, S, LANES), lambda b, p: (b, 0, npair + p)),
            pl.BlockSpec((None, S, LANES), lambda b, p: (b, 0, 2 * npair + p)),
            pl.BlockSpec((2, NA_WIN_H, GRID_W, NA_WIN_H * GRID_W), lambda b, p: (p, 0, 0, 0)),
        ],
        out_specs=pl.BlockSpec((None, S, LANES), lambda b, p: (b, 0, p)),
        compiler_params=_cparams(("arbitrary", "arbitrary"), VMEM_LIMIT),
        name="na_attention",
    )(proj, proj, proj, bias_tab)


def _rms(x, g):
    xf = x.astype(_F32)
    return xf * lax.rsqrt(jnp.mean(xf * xf, axis=-1, keepdims=True) + RMS_EPS) * g


def _mla_prep_kernel(lat_ref, qn_ref, kn_ref, wq_ref, wk_ref, wv_ref, cos_ref, sin_ref,
                     q_ref, k_ref, v_ref):
    lat = lat_ref[...]
    qn = _rms(lat[:, :MLA_Q_RANK], qn_ref[...]).astype(_BF16)
    kvn = _rms(lat[:, MLA_Q_RANK:MLA_Q_RANK + MLA_KV_RANK], kn_ref[...]).astype(_BF16)
    kr = lat[:, MLA_Q_RANK + MLA_KV_RANK:].astype(_F32)
    cos = cos_ref[...]
    sin = sin_ref[...]
    lane = lax.broadcasted_iota(jnp.int32, cos.shape, 1)
    half = MLA_ROPE // 2
    first = lane < MLA_NOPE + half
    scale = (MLA_NOPE + MLA_ROPE) ** -0.5
    kpe = kr * cos + _rope_rotate(kr, half, first) * sin
    q_all = _dotf(qn, wq_ref[...])
    k_all = _dotf(kvn, wk_ref[...])
    v_all = _dotf(kvn, wv_ref[...])
    ones_hi = jnp.where(lane >= MLA_V, 1.0, 0.0)
    for h in range(MLA_HEADS):
        sl = slice(h * LANES, (h + 1) * LANES)
        qh = q_all[:, sl]
        qh = (qh * cos + _rope_rotate(qh, half, first) * sin) * scale
        q_ref[:, sl] = qh.astype(q_ref.dtype)
        k_ref[:, sl] = (k_all[:, sl] + kpe).astype(k_ref.dtype)
        v_ref[:, sl] = (v_all[:, sl] + ones_hi).astype(v_ref.dtype)


def _mla_prep(proj, col_block, qn, kn, wq, wk, wv, cos, sin, tm):
    B, S, _ = proj.shape
    wlat = MLA_Q_RANK + MLA_KV_RANK + LANES
    W = MLA_HEADS * LANES
    tm = min(tm, S)
    out = jax.ShapeDtypeStruct((B, S, W), _BF16)
    full = lambda shp: pl.BlockSpec(shp, lambda b, i: (0,) * len(shp))
    return pl.pallas_call(
        _mla_prep_kernel,
        out_shape=(out, out, out),
        grid=(B, S // tm),
        in_specs=[
            pl.BlockSpec((None, tm, wlat), lambda b, i: (b, i, col_block)),
            full((1, MLA_Q_RANK)), full((1, MLA_KV_RANK)),
            full(wq.shape), full(wk.shape), full(wv.shape),
            pl.BlockSpec((tm, LANES), lambda b, i: (i, 0)),
            pl.BlockSpec((tm, LANES), lambda b, i: (i, 0)),
        ],
        out_specs=(pl.BlockSpec((None, tm, W), lambda b, i: (b, i, 0)),) * 3,
        compiler_params=_cparams(("arbitrary", "arbitrary"), VMEM_LIMIT),
        name="mla_prep",
    )(proj, qn, kn, wq, wk, wv, cos, sin)


def _flash_kernel(q_ref, k_ref, v_ref, o_ref, m_ref, acc_ref, *, tk):
    S = k_ref.shape[0]
    q = q_ref[...]
    m_ref[...] = jnp.full(m_ref.shape, MASK_VALUE, _F32)
    acc_ref[...] = jnp.zeros(acc_ref.shape, _F32)

    def kv_step(i, carry):
        start = pl.multiple_of(i * tk, tk)
        s = _dot_nt(q, k_ref[pl.ds(start, tk), :])
        m_old = m_ref[...]
        m_new = jnp.maximum(m_old, jnp.max(s, axis=1, keepdims=True))
        p = jnp.exp(s - m_new)
        acc_ref[...] = acc_ref[...] * jnp.exp(m_old - m_new) + _dotf(p.astype(_BF16), v_ref[pl.ds(start, tk), :])
        m_ref[...] = m_new
        return carry

    lax.fori_loop(0, S // tk, kv_step, 0)
    acc = acc_ref[...]
    lane = lax.broadcasted_iota(jnp.int32, acc.shape, 1)
    den = jnp.where(lane < MLA_V, pltpu.roll(acc, MLA_V, 1), acc)
    o_ref[...] = (acc / den).astype(o_ref.dtype)


def _mla_flash(q, k, v, tq, tk):
    B, S, W = q.shape
    tq = min(tq, S)
    tk = min(tk, S)
    kern = functools.partial(_flash_kernel, tk=tk)
    return pl.pallas_call(
        kern,
        out_shape=jax.ShapeDtypeStruct((B, S, W), _BF16),
        grid=(B, MLA_HEADS, S // tq),
        in_specs=[
            pl.BlockSpec((None, tq, LANES), lambda b, h, i: (b, i, h)),
            pl.BlockSpec((None, S, LANES), lambda b, h, i: (b, 0, h)),
            pl.BlockSpec((None, S, LANES), lambda b, h, i: (b, 0, h)),
        ],
        out_specs=pl.BlockSpec((None, tq, LANES), lambda b, h, i: (b, i, h)),
        scratch_shapes=[pltpu.VMEM((tq, 1), _F32), pltpu.VMEM((tq, LANES), _F32)],
        compiler_params=_cparams(("arbitrary",) * 3, VMEM_LIMIT),
        name="mla_flash",
    )(q, k, v)


DIL_QBLK = 128
DIL_HALF = 64
DIL_KWIN = DIL_QBLK + 2 * DIL_HALF


def _dil_kernel(q_ref, k_ref, v_ref, o_ref, lse_ref, *, L):
    lane = lax.broadcasted_iota(jnp.int32, (DIL_QBLK, LANES), 1)
    low = lane < HEAD_DIM
    qi = lax.broadcasted_iota(jnp.int32, (DIL_QBLK, DIL_KWIN), 0)
    ki = lax.broadcasted_iota(jnp.int32, (DIL_QBLK, DIL_KWIN), 1)

    def blk_step(i, carry):
        q0 = pl.multiple_of(i * DIL_QBLK, DIL_QBLK)
        start = pl.multiple_of(jnp.clip(q0 - DIL_HALF, 0, L - DIL_KWIN), DIL_HALF)
        q = q_ref[pl.ds(q0, DIL_QBLK), :]
        kw = k_ref[pl.ds(start, DIL_KWIN), :]
        vw = v_ref[pl.ds(start, DIL_KWIN), :]
        diff = (ki + start) - (qi + q0)
        valid = jnp.abs(diff) <= DIL_HALF
        outs, lses = [], []
        for hh in range(2):
            sel = low if hh == 0 else jnp.logical_not(low)
            qh = jnp.where(sel, q, jnp.zeros_like(q)) * jnp.asarray(HEAD_DIM ** -0.5, q.dtype)
            s = jnp.where(valid, _dot_nt(qh, kw), MASK_VALUE)
            m = jnp.max(s, axis=1, keepdims=True)
            p = jnp.exp(s - m)
            l = jnp.sum(p, axis=1, keepdims=True)
            outs.append(_dotf(p.astype(_BF16), vw) / l)
            lses.append(m + jnp.log(l))
        o_ref[pl.ds(q0, DIL_QBLK), :] = jnp.where(low, outs[0], outs[1]).astype(o_ref.dtype)
        lse_ref[pl.ds(q0, DIL_QBLK), :] = jnp.where(low, lses[0], lses[1])
        return carry

    lax.fori_loop(0, L // DIL_QBLK, blk_step, 0)


def _dil_attention(qkv, S):
    B, d, L, _ = qkv.shape
    assert L >= DIL_KWIN and L % DIL_QBLK == 0
    npair = DIL_HEADS // 2
    kern = functools.partial(_dil_kernel, L=L)
    o, lse = pl.pallas_call(
        kern,
        out_shape=(jax.ShapeDtypeStruct((B, L, d * DIL_WIDTH), _BF16),
                   jax.ShapeDtypeStruct((B, L, d * DIL_WIDTH), _F32)),
        grid=(B, d, npair),
        in_specs=[
            pl.BlockSpec((None, None, L, LANES), lambda b, r, p: (b, r, 0, p)),
            pl.BlockSpec((None, None, L, LANES), lambda b, r, p: (b, r, 0, npair + p)),
            pl.BlockSpec((None, None, L, LANES), lambda b, r, p: (b, r, 0, 2 * npair + p)),
        ],
        out_specs=(pl.BlockSpec((None, L, LANES), lambda b, r, p: (b, 0, r * npair + p)),) * 2,
        compiler_params=_cparams(("arbitrary",) * 3, VMEM_LIMIT),
        name="dil_attention",
    )(qkv, qkv, qkv)
    return o.reshape(B, S, DIL_WIDTH), lse.reshape(B, S, DIL_WIDTH)


def _post_mixer(out, x_ref, gate_ref, g_ref, b_ref, shift2_ref, scale2_ref, wr_ref,
                xo_ref, h_ref, aff_ref):
    y = DEEPNORM_ALPHA * x_ref[...] + gate_ref[...] * out
    mu = jnp.mean(y, axis=-1, keepdims=True)
    yc = y - mu
    var = jnp.mean(yc * yc, axis=-1, keepdims=True)
    xn = yc * lax.rsqrt(var + LN_EPS) * g_ref[...] + b_ref[...]
    xo_ref[...] = xn
    h = xn * (1.0 + scale2_ref[...]) + shift2_ref[...]
    h_ref[...] = h
    hh, hl = _split2(h)
    wh, wl = _split2(wr_ref[...])
    logits = _dot_nt(wh, hh) + (_dot_nt(wh, hl) + _dot_nt(wl, hh))
    mx = jnp.max(logits, axis=0, keepdims=True)
    ex = jnp.exp(logits - mx)
    aff_ref[...] = ex / jnp.sum(ex, axis=0, keepdims=True)


def _outproj_ab_kernel(a1_ref, a2_ref, w1_ref, w2_ref, *rest):
    out = _dotf(a1_ref[...], w1_ref[...]) + _dotf(a2_ref[...], w2_ref[...])
    _post_mixer(out, *rest)


def _outproj_dil_kernel(o1, l1, o2, l2, o3, l3, w_ref, *rest):
    ls = [l1[...], l2[...], l3[...]]
    mx = jnp.maximum(jnp.maximum(ls[0], ls[1]), ls[2])
    ws = [jnp.exp(l - mx) for l in ls]
    den = ws[0] + ws[1] + ws[2]
    mix = sum((w / den) * o[...].astype(_F32) for w, o in zip(ws, (o1, o2, o3)))
    out = _dotf(mix.astype(_BF16), w_ref[...])
    _post_mixer(out, *rest)


def _outproj(kind, acts, weights, x, gate, ln_g, ln_b, shift2, scale2, wr_t, tm):
    B, S, D = x.shape
    E = wr_t.shape[0]
    tm = min(tm, S)
    nt = S // tm
    tok = lambda w: pl.BlockSpec((None, tm, w), lambda b, i: (b, i, 0))
    per_b = pl.BlockSpec((None, 1, D), lambda b, i: (b, 0, 0))
    full = lambda shp: pl.BlockSpec(shp, lambda b, i: (0,) * len(shp))
    in_specs = [tok(a.shape[-1]) for a in acts] + [full(w.shape) for w in weights]
    in_specs += [tok(D), per_b, full((1, D)), full((1, D)), per_b, per_b, full(wr_t.shape)]
    kern = _outproj_ab_kernel if kind == "ab" else _outproj_dil_kernel
    return pl.pallas_call(
        kern,
        out_shape=(jax.ShapeDtypeStruct((B, S, D), _F32),
                   jax.ShapeDtypeStruct((B, S, D), _F32),
                   jax.ShapeDtypeStruct((E, B * S), _F32)),
        grid=(B, nt),
        in_specs=in_specs,
        out_specs=(tok(D), tok(D), pl.BlockSpec((E, tm), lambda b, i: (0, b * nt + i))),
        compiler_params=_cparams(("arbitrary", "arbitrary"), VMEM_LIMIT),
        name="outproj_" + kind,
    )(*acts, *weights, x, gate, ln_g, ln_b, shift2, scale2, wr_t)


def _excl_cumsum_rows(x_bf, R):
    li = lax.broadcasted_iota(jnp.int32, (LANES, LANES), 0)
    lj = lax.broadcasted_iota(jnp.int32, (LANES, LANES), 1)
    upper = jnp.where(li < lj, 1.0, 0.0).astype(_BF16)
    ones = jnp.ones((LANES, LANES), _BF16)
    within = _dotf(x_bf, upper)
    rowtot = _dotf(x_bf, ones)
    ri = lax.broadcasted_iota(jnp.int32, (R, R), 0)
    rj = lax.broadcasted_iota(jnp.int32, (R, R), 1)
    lower = jnp.where(rj < ri, 1.0, 0.0).astype(_BF16)
    return within + _dotf(lower, rowtot.astype(_BF16))


def _select_kernel(aff_ref, cs_ref, cnt_ref, *, cap):
    aff = aff_ref[...]
    R = aff.shape[0]
    bits = pltpu.bitcast(aff, jnp.int32)
    capf = jnp.float32(cap)

    def count_ge(t):
        return jnp.sum(jnp.where(bits >= t, 1.0, 0.0))

    def search(_, lohi):
        lo, hi = lohi
        mid = lo + lax.shift_right_logical(hi - lo, 1)
        ok = count_ge(mid) >= capf
        return jnp.where(ok, mid, lo), jnp.where(ok, hi, mid)

    lo, _ = lax.fori_loop(0, 31, search, (jnp.int32(0), jnp.int32(0x7F800000)))
    gt = bits > lo
    eq = bits == lo
    need = capf - jnp.sum(jnp.where(gt, 1.0, 0.0))
    eq_rank = _excl_cumsum_rows(jnp.where(eq, 1.0, 0.0).astype(_BF16), R)
    sel = jnp.logical_or(gt, jnp.logical_and(eq, eq_rank < need))
    cs = _excl_cumsum_rows(jnp.where(sel, 1.0, 0.0).astype(_BF16), R)
    cs_ref[...] = jnp.where(sel, cs, -1.0)
    cnt_ref[...] = cs.astype(jnp.int32)


def _moe_select(aff_t, cap):
    E, N = aff_t.shape
    R = N // LANES
    kern = functools.partial(_select_kernel, cap=cap)
    spec = pl.BlockSpec((None, R, LANES), lambda e: (e, 0, 0))
    slot, before = pl.pallas_call(
        kern,
        out_shape=(jax.ShapeDtypeStruct((E, R, LANES), _F32), jax.ShapeDtypeStruct((E, R, LANES), jnp.int32)),
        grid=(E,),
        in_specs=[spec],
        out_specs=(spec, spec),
        compiler_params=_cparams(("arbitrary",), VMEM_LIMIT),
        name="moe_select",
    )(aff_t.reshape(E, R, LANES))
    return slot.reshape(E, N), before.reshape(E, N)


COMPACT_TILE = 256


def _compact_kernel(slot_ref, aff_ref, idx_ref, gate_ref):
    k = pl.program_id(1)
    slot = slot_ref[...]
    R = slot.shape[0]
    T = COMPACT_TILE
    m_bf = jnp.where(slot >= 0.0, 1.0, 0.0).astype(_BF16)
    ones8 = jnp.ones((8, LANES), _BF16)
    rowtot = _dot_nt(ones8, m_bf)
    ri = lax.broadcasted_iota(jnp.int32, (R, R), 0)
    rj = lax.broadcasted_iota(jnp.int32, (R, R), 1)
    incl = _dotf(rowtot.astype(_BF16), jnp.where(ri <= rj, 1.0, 0.0).astype(_BF16))
    excl = incl - rowtot
    c = (k * T + lax.broadcasted_iota(jnp.int32, (T, R), 0)).astype(_F32)
    oh = jnp.logical_and(excl[0:1, :] <= c, c < incl[0:1, :])
    ohb = jnp.where(oh, 1.0, 0.0).astype(_BF16)
    li = lax.broadcasted_iota(jnp.int32, (LANES, LANES), 0)
    lj = lax.broadcasted_iota(jnp.int32, (LANES, LANES), 1)
    loc_incl = _dotf(m_bf, jnp.where(li <= lj, 1.0, 0.0).astype(_BF16))
    a_hi, a_mid, a_lo = _split3(aff_ref[...])
    g_cs = _dotf(ohb, loc_incl.astype(_BF16))
    g_aff = _dotf(ohb, a_hi) + (_dotf(ohb, a_mid) + _dotf(ohb, a_lo))
    excl_sel = jnp.sum(jnp.where(oh, excl[0:1, :], 0.0), axis=1, keepdims=True)
    row_sel = jnp.sum(jnp.where(oh, rj[0:1, :].astype(_F32), 0.0), axis=1, keepdims=True)
    target = c[:, 0:1] - excl_sel + 1.0
    lane_sel = jnp.sum(jnp.where(g_cs < target, 1.0, 0.0), axis=1, keepdims=True)
    lane_f = lax.broadcasted_iota(jnp.int32, (T, LANES), 1).astype(_F32)
    gate = jnp.sum(jnp.where(lane_f == lane_sel, g_aff, 0.0), axis=1, keepdims=True)
    idx_ref[...] = jnp.broadcast_to(row_sel * LANES + lane_sel, (T, LANES)).astype(jnp.int32)
    gate_ref[...] = jnp.broadcast_to(gate, (T, LANES))


def _moe_compact(slot, aff_t, cap):
    E, N = slot.shape
    R = N // LANES
    T = COMPACT_TILE
    nt = cap // T
    spec_in = pl.BlockSpec((None, R, LANES), lambda e, k: (e, 0, 0))
    spec_out = pl.BlockSpec((T, LANES), lambda e, k: (e * nt + k, 0))
    return pl.pallas_call(
        _compact_kernel,
        out_shape=(jax.ShapeDtypeStruct((E * cap, LANES), jnp.int32),
                   jax.ShapeDtypeStruct((E * cap, LANES), _F32)),
        grid=(E, nt),
        in_specs=[spec_in, spec_in],
        out_specs=(spec_out, spec_out),
        compiler_params=_cparams(("arbitrary", "arbitrary"), VMEM_LIMIT),
        name="moe_compact",
    )(slot.reshape(E, R, LANES), aff_t.reshape(E, R, LANES))


def _row_copy(h_hbm, xbuf, sem, src_row, dst_row):
    return pltpu.make_async_copy(h_hbm.at[pl.ds(src_row, 1)], xbuf.at[pl.ds(dst_row, 1)], sem)


def _ffn_kernel(idx_ref, gate_ref, wg_ref, wu_ref, wd_ref, h_hbm, o_ref, xbuf, sem, *, tm):
    t = pl.program_id(1)
    base = t * tm

    def issue(i, carry):
        _row_copy(h_hbm, xbuf, sem, idx_ref[base + i], i).start()
        return carry

    lax.fori_loop(0, tm, issue, 0)

    def drain(i, carry):
        _row_copy(h_hbm, xbuf, sem, 0, i).wait()
        return carry

    lax.fori_loop(0, tm, drain, 0)
    x = xbuf[...].astype(_BF16)
    g = _dotf(x, wg_ref[...])
    u = _dotf(x, wu_ref[...])
    mid = (_silu(g) * u).astype(_BF16)
    y = _dotf(mid, wd_ref[...])
    gate = gate_ref[...]
    for j in range(y.shape[1] // LANES):
        o_ref[:, j * LANES:(j + 1) * LANES] = (y[:, j * LANES:(j + 1) * LANES] * gate).astype(o_ref.dtype)


def _moe_ffn(idx, gate, h, wg, wu, wd, cap, tm):
    E, D, F = wg.shape
    tm = min(tm, cap)
    nt = cap // tm
    kern = functools.partial(_ffn_kernel, tm=tm)
    return pl.pallas_call(
        kern,
        out_shape=jax.ShapeDtypeStruct((E * cap, D), _BF16),
        grid=(E, nt),
        in_specs=[
            pl.BlockSpec((cap,), lambda e, t: (e,), memory_space=pltpu.SMEM),
            pl.BlockSpec((tm, LANES), lambda e, t: (e * nt + t, 0)),
            pl.BlockSpec((None, D, F), lambda e, t: (e, 0, 0)),
            pl.BlockSpec((None, D, F), lambda e, t: (e, 0, 0)),
            pl.BlockSpec((None, F, D), lambda e, t: (e, 0, 0)),
            pl.BlockSpec(memory_space=pl.ANY),
        ],
        out_specs=pl.BlockSpec((tm, D), lambda e, t: (e * nt + t, 0)),
        scratch_shapes=[pltpu.VMEM((tm, D), _F32), pltpu.SemaphoreType.DMA],
        compiler_params=_cparams(("arbitrary", "arbitrary"), VMEM_LIMIT),
        name="moe_ffn",
    )(idx, gate, wg, wu, wd, h)


COMBINE_TOK = 256
COMBINE_WIN = 128


def _window_copy(ye_hbm, ybuf, sems, e, row0, slot):
    return pltpu.make_async_copy(ye_hbm.at[pl.ds(row0, COMBINE_WIN)], ybuf.at[slot], sems.at[slot])


def _combine_kernel(bstart_ref, slot_ref, x_ref, gate_ref, g_ref, b_ref, ye_hbm, o_ref,
                    ybuf, xtra, sems, xsem, acc_ref, *, cap, n_exp):
    blk = pl.program_id(0)
    W = COMBINE_WIN

    def window(e, j):
        s0 = bstart_ref[blk * n_exp + e]
        al = lax.shift_left(lax.shift_right_logical(s0, 4), 4)
        return e * cap + jnp.minimum(al + j * W, cap - W)

    for e in range(n_exp):
        _window_copy(ye_hbm, ybuf, sems, e, pl.multiple_of(window(e, 0), BF16_SUBLANES), e).start()
    acc_ref[...] = jnp.zeros(acc_ref.shape, _F32)
    slots = slot_ref[...]
    lane = lax.broadcasted_iota(jnp.int32, (COMBINE_TOK, W), 1)

    def place(e, row0, first_slot, rows):
        slot_e = slots[:, e:e + 1].astype(jnp.int32)
        local = jnp.where(slot_e >= first_slot, slot_e - (row0 - e * cap), -1)
        onehot = jnp.where(local == lane, 1.0, 0.0).astype(_BF16)
        acc_ref[...] += _dotf(onehot, rows)

    for e in range(n_exp):
        row0 = pl.multiple_of(window(e, 0), BF16_SUBLANES)
        _window_copy(ye_hbm, ybuf, sems, e, row0, e).wait()
        place(e, row0, 0, ybuf[e])
        s0 = bstart_ref[blk * n_exp + e]
        s1 = bstart_ref[(blk + 1) * n_exp + e]
        al = lax.shift_left(lax.shift_right_logical(s0, 4), 4)
        nwin = lax.div(s1 - al + (W - 1), W)

        def extra(j, carry, e=e, al=al):
            r0 = pl.multiple_of(window(e, j), BF16_SUBLANES)
            cp = pltpu.make_async_copy(ye_hbm.at[pl.ds(r0, W)], xtra, xsem)
            cp.start()
            cp.wait()
            place(e, r0, al + j * W, xtra[...])
            return carry

        lax.fori_loop(1, nwin, extra, 0)

    y = DEEPNORM_ALPHA * x_ref[...] + gate_ref[...] * acc_ref[...]
    mu = jnp.mean(y, axis=-1, keepdims=True)
    yc = y - mu
    var = jnp.mean(yc * yc, axis=-1, keepdims=True)
    o_ref[...] = yc * lax.rsqrt(var + LN_EPS) * g_ref[...] + b_ref[...]


def _moe_combine(bstart, slot_t, x, gate, ln_g, ln_b, ye, cap):
    B, S, D = x.shape
    N = B * S
    E = slot_t.shape[1]
    T = COMBINE_TOK
    per_seq = S // T
    kern = functools.partial(_combine_kernel, cap=cap, n_exp=E)
    out = pl.pallas_call(
        kern,
        out_shape=jax.ShapeDtypeStruct((N, D), _F32),
        grid_spec=pltpu.PrefetchScalarGridSpec(
            num_scalar_prefetch=1,
            grid=(N // T,),
            in_specs=[
                pl.BlockSpec((T, E), lambda i, bs: (i, 0)),
                pl.BlockSpec((T, D), lambda i, bs: (i, 0)),
                pl.BlockSpec((None, 1, D), lambda i, bs: (i // per_seq, 0, 0)),
                pl.BlockSpec((1, D), lambda i, bs: (0, 0)),
                pl.BlockSpec((1, D), lambda i, bs: (0, 0)),
                pl.BlockSpec(memory_space=pl.ANY),
            ],
            out_specs=pl.BlockSpec((T, D), lambda i, bs: (i, 0)),
            scratch_shapes=[
                pltpu.VMEM((E, COMBINE_WIN, D), _BF16),
                pltpu.VMEM((COMBINE_WIN, D), _BF16),
                pltpu.SemaphoreType.DMA((E,)),
                pltpu.SemaphoreType.DMA,
                pltpu.VMEM((T, D), _F32),
            ],
        ),
        compiler_params=_cparams(("arbitrary",), VMEM_LIMIT),
        name="moe_combine",
    )(bstart, slot_t, x.reshape(N, D), gate, ln_g, ln_b, ye)
    return out.reshape(B, S, D)


def _expert_choice(x1, h2, aff_t, gate, ln_g, ln_b, wg, wu, wd, ffn_tm):
    B, S, D = x1.shape
    N = B * S
    E = aff_t.shape[0]
    cap = EC_CAPACITY * N // E
    slot, before = _moe_select(aff_t, cap)
    idx_rep, gate_rep = _moe_compact(slot, aff_t, cap)
    ye = _moe_ffn(idx_rep[:, 0], gate_rep, h2.reshape(N, D), wg, wu, wd, cap, ffn_tm)
    bstart = jnp.concatenate([before[:, ::COMBINE_TOK], jnp.full((E, 1), cap, jnp.int32)], axis=1)
    return _moe_combine(bstart.T.reshape(-1), slot.T, x1, gate, ln_g, ln_b, ye, cap)


def _rope_np(n, dim):
    inv = 1.0 / (ROPE_THETA ** (jnp.arange(0, dim, 2, dtype=_F32) / dim))
    ang = jnp.arange(n, dtype=_F32)[:, None] * inv[None, :]
    return jnp.cos(ang), jnp.sin(ang)


def _head_rope_tables(S):
    cos, sin = _rope_np(S, HEAD_DIM)
    c = jnp.tile(cos, (1, LANES // (HEAD_DIM // 2)))
    s = jnp.tile(jnp.concatenate([-sin, sin], axis=1), (1, LANES // HEAD_DIM))
    return c, s


def _mla_rope_tables(S):
    cos, sin = _rope_np(S, MLA_ROPE)
    pad = LANES - MLA_NOPE - MLA_ROPE
    c = jnp.concatenate([jnp.ones((S, MLA_NOPE), _F32), cos, cos, jnp.zeros((S, pad), _F32)], axis=1)
    s = jnp.concatenate([jnp.zeros((S, MLA_NOPE), _F32), -sin, sin, jnp.zeros((S, pad), _F32)], axis=1)
    return c, s


def _prep_ab_weights(w_in, w_q_up, w_kv_up, w_out):
    D = w_in.shape[0]
    s1 = 3 * NA_WIDTH
    s3 = s1 + MLA_Q_RANK + MLA_KV_RANK
    pad = LANES - MLA_NOPE - MLA_ROPE
    krope = jnp.concatenate([jnp.zeros((D, MLA_NOPE), _F32), w_in[:, s3:], jnp.zeros((D, pad), _F32)], axis=1)
    w_in_p = jnp.concatenate([w_in[:, :s3], krope], axis=1).astype(_BF16)
    wq = w_q_up.reshape(MLA_Q_RANK, MLA_HEADS, MLA_NOPE + MLA_ROPE)
    wq = jnp.pad(wq, ((0, 0), (0, 0), (0, pad))).reshape(MLA_Q_RANK, MLA_HEADS * LANES).astype(_BF16)
    wkv = w_kv_up.reshape(MLA_KV_RANK, MLA_HEADS, MLA_NOPE + MLA_V)
    wk = jnp.pad(wkv[:, :, :MLA_NOPE], ((0, 0), (0, 0), (0, LANES - MLA_NOPE)))
    wv = jnp.pad(wkv[:, :, MLA_NOPE:], ((0, 0), (0, 0), (0, LANES - MLA_V)))
    wk = wk.reshape(MLA_KV_RANK, MLA_HEADS * LANES).astype(_BF16)
    wv = wv.reshape(MLA_KV_RANK, MLA_HEADS * LANES).astype(_BF16)
    w_out_na = w_out[:NA_WIDTH].astype(_BF16)
    wo = w_out[NA_WIDTH:].reshape(MLA_HEADS, MLA_V, D)
    w_out_mla = jnp.pad(wo, ((0, 0), (0, LANES - MLA_V), (0, 0))).reshape(MLA_HEADS * LANES, D).astype(_BF16)
    return w_in_p, wq, wk, wv, w_out_na, w_out_mla


def _trunk(x, c_mods, P, layer_w):
    B, S, D = x.shape
    cos_h, sin_h = _head_rope_tables(S)
    cos_r, sin_r = _mla_rope_tables(S)
    for layer in range(DEPTH):
        j = layer // 2
        mm = c_mods[layer]
        lw = layer_w[layer]
        if layer % 2 == 0:
            proj = _inproj(x, mm["mix_shift"], mm["mix_scale"], lw["w_in"], cos_h, sin_h,
                           dil=1, rope_cols=0, half=1, tm=512).reshape(B, S, -1)
            na = _na_attention(proj, lw["bias_tab"])
            wlat = MLA_Q_RANK + MLA_KV_RANK + LANES
            q, k, v = _mla_prep(proj, (3 * NA_WIDTH) // wlat, lw["q_norm"], lw["kv_norm"],
                                lw["wq"], lw["wk"], lw["wv"], cos_r, sin_r, tm=512)
            mla = _mla_flash(q, k, v, tq=512, tk=512)
            acts, weights, kind = [na, mla], [lw["w_out_na"], lw["w_out_mla"]], "ab"
        else:
            acts = []
            for g, (window, dil) in enumerate(DIL_PAIRS):
                assert (window // 2) // dil == DIL_HALF
                qkv = _inproj(x, mm["mix_shift"], mm["mix_scale"], lw["w_in"][g], cos_h, sin_h,
                              dil=dil, rope_cols=2 * DIL_WIDTH, half=HEAD_DIM // 2, tm=512)
                o, lse = _dil_attention(qkv, S)
                acts += [o, lse]
            weights, kind = [lw["w_out"]], "dil"
        x1, h2, aff_t = _outproj(kind, acts, weights, x, mm["mix_gate"], lw["mix_ln_g"], lw["mix_ln_b"],
                                 mm["moe_shift"], mm["moe_scale"], lw["wr_t"], tm=256)
        x = _expert_choice(x1, h2, aff_t, mm["moe_gate"], lw["moe_ln_g"], lw["moe_ln_b"],
                           lw["wg"], lw["wu"], lw["wd"], ffn_tm=256)
    return x


def kernel(x_prompt, x_sample, c_prompt, c_sample, ab_w_in, na_rpb, mla_q_norm, mla_w_q_up, mla_kv_norm,
           mla_w_kv_up, ab_w_out, dil_w_in, dil_w_out, mix_ada_w, mix_ada_b, mix_ln_g, mix_ln_b,
           moe_ada_w, moe_ada_b, moe_ln_g, moe_ln_b, moe_w_router, moe_w_gate, moe_w_up, moe_w_down):
    D = x_prompt.shape[-1]
    nb_p, nb_s = c_prompt.shape[0], c_sample.shape[0]
    bp = -(-(nb_p + nb_s) // 8) * 8
    c_all = jnp.zeros((bp, D), _F32).at[:nb_p].set(c_prompt).at[nb_p:nb_p + nb_s].set(c_sample)
    mix_mod = _ada_mod(c_all, mix_ada_w, mix_ada_b)
    moe_mod = _ada_mod(c_all, moe_ada_w, moe_ada_b)

    def mods_for(lo, n):
        out = []
        for layer in range(DEPTH):
            d = {}
            for name, mod in (("mix", mix_mod), ("moe", moe_mod)):
                m = mod[layer, lo:lo + n]
                d[name + "_shift"] = m[:, None, :D]
                d[name + "_scale"] = m[:, None, D:2 * D]
                d[name + "_gate"] = m[:, None, 2 * D:]
            out.append(d)
        return out

    layer_w = []
    for layer in range(DEPTH):
        j = layer // 2
        lw = {
            "mix_ln_g": mix_ln_g[layer][None], "mix_ln_b": mix_ln_b[layer][None],
            "moe_ln_g": moe_ln_g[layer][None], "moe_ln_b": moe_ln_b[layer][None],
            "wr_t": moe_w_router[layer].T,
            "wg": moe_w_gate[layer].astype(_BF16), "wu": moe_w_up[layer].astype(_BF16),
            "wd": moe_w_down[layer].astype(_BF16),
        }
        if layer % 2 == 0:
            w_in_p, wq, wk, wv, w_out_na, w_out_mla = _prep_ab_weights(
                ab_w_in[j], mla_w_q_up[j], mla_w_kv_up[j], ab_w_out[j])
            lw.update(w_in=w_in_p, wq=wq, wk=wk, wv=wv, w_out_na=w_out_na, w_out_mla=w_out_mla,
                      q_norm=mla_q_norm[j][None], kv_norm=mla_kv_norm[j][None],
                      bias_tab=_na_bias_table(na_rpb[j]))
        else:
            wi = dil_w_in[j].reshape(D, len(DIL_PAIRS), 3 * DIL_WIDTH).astype(_BF16)
            lw.update(w_in=[wi[:, g] for g in range(len(DIL_PAIRS))], w_out=dil_w_out[j].astype(_BF16))
        layer_w.append(lw)

    y_prompt = _trunk(x_prompt, mods_for(0, nb_p), None, layer_w)
    y_sample = _trunk(x_sample, mods_for(nb_p, nb_s), None, layer_w)
    return (y_prompt, y_sample)
```

```python
import functools

import numpy as np
import jax
import jax.numpy as jnp
from jax import lax
from jax.experimental import pallas as pl
from jax.experimental.pallas import tpu as pltpu

GRID_W = 64
HEAD_DIM = 64
NA_HEADS = 8
NA_WIN_H = 8
NA_WIN_W = 16
MLA_HEADS = 8
MLA_Q_RANK = 256
MLA_KV_RANK = 128
MLA_NOPE = 64
MLA_ROPE = 32
MLA_V = 64
DIL_PAIRS = ((128, 1), (512, 4), (2048, 16))
DIL_HEADS = 8
N_EXPERTS = 16
EC_CAPACITY = 2
ROPE_THETA = 10000.0
LN_EPS = 1e-5
RMS_EPS = 1e-6
DEPTH = 2
DEEPNORM_ALPHA = (2 * DEPTH) ** 0.25
NA_WIDTH = NA_HEADS * HEAD_DIM
DIL_WIDTH = DIL_HEADS * HEAD_DIM

LANES = 128
BF16_SUBLANES = 16
MASK_VALUE = -1e30
VMEM_LIMIT = 56 * 1024 * 1024

_F32 = jnp.float32
_BF16 = jnp.bfloat16


def _cparams(sem, vmem=None):
    return pltpu.CompilerParams(dimension_semantics=sem, vmem_limit_bytes=vmem)


def _dotf(a, b):
    return jnp.dot(a, b, preferred_element_type=_F32)


def _dot_nt(a, b):
    return lax.dot_general(a, b, (((1,), (1,)), ((), ())), preferred_element_type=_F32)


def _split2(x):
    hi = x.astype(_BF16)
    lo = (x - hi.astype(_F32)).astype(_BF16)
    return hi, lo


def _split3(x):
    hi = x.astype(_BF16)
    r = x - hi.astype(_F32)
    mid = r.astype(_BF16)
    lo = (r - mid.astype(_F32)).astype(_BF16)
    return hi, mid, lo


def _silu(x):
    return x / (1.0 + jnp.exp(-x))


def _rope_rotate(x, half, first_half):
    n = x.shape[-1]
    fwd = pltpu.roll(x, n - half, 1)
    bwd = pltpu.roll(x, half, 1)
    return jnp.where(first_half, fwd, bwd)


def _ada_kernel(c_ref, w_ref, b_ref, o_ref):
    a = _silu(c_ref[...])
    ah, al = _split2(a)
    wh, wl = _split2(w_ref[...])
    o_ref[...] = _dotf(ah, wh) + (_dotf(ah, wl) + _dotf(al, wh)) + b_ref[...]


def _ada_mod(c, w, b):
    L, D, N3 = w.shape
    Bp = c.shape[0]
    tn = min(512, N3)
    return pl.pallas_call(
        _ada_kernel,
        out_shape=jax.ShapeDtypeStruct((L, Bp, N3), _F32),
        grid=(L, N3 // tn),
        in_specs=[
            pl.BlockSpec((Bp, D), lambda l, j: (0, 0)),
            pl.BlockSpec((None, D, tn), lambda l, j: (l, 0, j)),
            pl.BlockSpec((None, 1, tn), lambda l, j: (l, 0, j)),
        ],
        out_specs=pl.BlockSpec((None, Bp, tn), lambda l, j: (l, 0, j)),
        compiler_params=_cparams(("arbitrary", "arbitrary")),
        name="ada_mod",
    )(c, w, b.reshape(L, 1, N3))


INPROJ_CHUNK = 2048


def _inproj_kernel(x_ref, shift_ref, scale_ref, w_ref, cos_ref, sin_ref, o_ref, *scratch,
                   dil, tm, rope_cols, half):
    if dil == 1:
        rows = slice(None)
        x = x_ref[...]
    else:
        (xs_ref,) = scratch
        nlt = xs_ref.shape[0]

        @pl.when(pl.program_id(2) == 0)
        def _():
            for c in range(nlt):
                xs_ref[c] = x_ref[:, c * LANES:(c + 1) * LANES]

        rows = pl.ds(pl.program_id(2), tm, stride=dil)
        x = jnp.concatenate([xs_ref[c, rows, :] for c in range(nlt)], axis=1)
    h = x * (1.0 + scale_ref[...]) + shift_ref[...]
    acc = _dotf(h.astype(_BF16), w_ref[...])
    if rope_cols:
        cos = cos_ref[rows, :]
        sin = sin_ref[rows, :]
        lane = lax.broadcasted_iota(jnp.int32, cos.shape, 1)
        first = (lane % (2 * half)) < half
        for j in range(rope_cols // LANES):
            blk = acc[:, j * LANES:(j + 1) * LANES]
            blk = blk * cos + _rope_rotate(blk, half, first) * sin
            o_ref[:, j * LANES:(j + 1) * LANES] = blk.astype(o_ref.dtype)
        o_ref[:, rope_cols:] = acc[:, rope_cols:].astype(o_ref.dtype)
    else:
        o_ref[...] = acc.astype(o_ref.dtype)


def _inproj(x, shift, scale, w, cos, sin, *, dil, rope_cols, half, tm):
    B, S, D = x.shape
    N = w.shape[1]
    L = S // dil
    tm = min(tm, L, INPROJ_CHUNK // dil)
    chunk = tm * dil
    kern = functools.partial(_inproj_kernel, dil=dil, tm=tm, rope_cols=rope_cols, half=half)
    return pl.pallas_call(
        kern,
        out_shape=jax.ShapeDtypeStruct((B, dil, L, N), _BF16),
        grid=(B, L // tm, dil),
        in_specs=[
            pl.BlockSpec((None, chunk, D), lambda b, j, r: (b, j, 0)),
            pl.BlockSpec((None, 1, D), lambda b, j, r: (b, 0, 0)),
            pl.BlockSpec((None, 1, D), lambda b, j, r: (b, 0, 0)),
            pl.BlockSpec((D, N), lambda b, j, r: (0, 0)),
            pl.BlockSpec((chunk, LANES), lambda b, j, r: (j, 0)),
            pl.BlockSpec((chunk, LANES), lambda b, j, r: (j, 0)),
        ],
        out_specs=pl.BlockSpec((None, None, tm, N), lambda b, j, r: (b, r, j, 0)),
        scratch_shapes=[pltpu.VMEM((D // LANES, chunk, LANES), _F32)] if dil > 1 else [],
        compiler_params=_cparams(("arbitrary",) * 3, VMEM_LIMIT),
        name="inproj",
    )(x, shift, scale, w, cos, sin)


def _na_kernel(q_ref, k_ref, v_ref, bias_ref, o_ref, *, rows):
    nk = NA_WIN_H * GRID_W
    lane = lax.broadcasted_iota(jnp.int32, (GRID_W, LANES), 1)
    low = lane < HEAD_DIM

    def row_step(r, carry):
        r0 = jnp.clip(r - NA_WIN_H // 2, 0, rows - NA_WIN_H)
        case = r - r0
        q = q_ref[pl.ds(pl.multiple_of(r * GRID_W, GRID_W), GRID_W), :]
        kw = k_ref[pl.ds(pl.multiple_of(r0 * GRID_W, GRID_W), nk), :]
        vw = v_ref[pl.ds(pl.multiple_of(r0 * GRID_W, GRID_W), nk), :]
        zero = jnp.zeros_like(q)
        q2 = jnp.concatenate([jnp.where(low, q, zero), jnp.where(low, zero, q)], axis=0)
        q2 = q2 * jnp.asarray(HEAD_DIM ** -0.5, q.dtype)
        s = _dot_nt(q2, kw) + bias_ref[:, case].reshape(2 * GRID_W, nk)
        m = jnp.max(s, axis=1, keepdims=True)
        p = jnp.exp(s - m)
        l = jnp.sum(p, axis=1, keepdims=True)
        o2 = _dotf(p.astype(_BF16), vw) / l
        o = jnp.where(low, o2[:GRID_W], o2[GRID_W:])
        o_ref[pl.ds(pl.multiple_of(r * GRID_W, GRID_W), GRID_W), :] = o.astype(o_ref.dtype)
        return carry

    lax.fori_loop(0, rows, row_step, 0, unroll=4)


def _na_bias_table(rpb):
    kh, kw = NA_WIN_H, NA_WIN_W
    H = rpb.shape[0]
    W = GRID_W
    P = 2 * W
    lead = (W - 1) - (kw - 1)
    row = jnp.pad(rpb.astype(_F32), ((0, 0), (0, 0), (lead, P - lead - (2 * kw - 1))))
    flat = jnp.tile(row, (1, 1, W))[:, :, :W * (P - 1)]
    toep = flat.reshape(H, 2 * kh - 1, W, P - 1)[:, :, :, W - 1:]
    cols = np.arange(W)
    c0 = np.clip(cols - kw // 2, 0, W - kw)
    valid = (cols[None, :] >= c0[:, None]) & (cols[None, :] < c0[:, None] + kw)
    toep = jnp.where(valid[None, None], toep, MASK_VALUE)
    per_case = [jnp.transpose(toep[:, kh - 1 - case:2 * kh - 1 - case], (0, 2, 1, 3)) for case in range(kh)]
    return jnp.stack(per_case, axis=1).reshape(H, kh, W, kh * W)


def _na_attention(proj, bias_tab):
    B, S, _ = proj.shape
    rows = S // GRID_W
    assert rows >= NA_WIN_H
    npair = NA_HEADS // 2
    kern = functools.partial(_na_kernel, rows=rows)
    return pl.pallas_call(
        kern,
        out_shape=jax.ShapeDtypeStruct((B, S, NA_WIDTH), _BF16),
        grid=(B, npair),
        in_specs=[
            pl.BlockSpec((None, S, LANES), lambda b, p: (b, 0, p)),
            pl.BlockSpec((None, S, LANES), lambda b, p: (b, 0, npair + p)),
            pl.BlockSpec((None, S, LANES), lambda b, p: (b, 0, 2 * npair + p)),
            pl.BlockSpec((2, NA_WIN_H, GRID_W, NA_WIN_H * GRID_W), lambda b, p: (p, 0, 0, 0)),
        ],
        out_specs=pl.BlockSpec((None, S, LANES), lambda b, p: (b, 0, p)),
        compiler_params=_cparams(("arbitrary", "arbitrary"), VMEM_LIMIT),
        name="na_attention",
    )(proj, proj, proj, bias_tab)


def _rms(x, g):
    xf = x.astype(_F32)
    return xf * lax.rsqrt(jnp.mean(xf * xf, axis=-1, keepdims=True) + RMS_EPS) * g


def _mla_prep_kernel(lat_ref, qn_ref, kn_ref, wq_ref, wk_ref, wv_ref, cos_ref, sin_ref,
                     q_ref, k_ref, v_ref):
    lat = lat_ref[...]
    qn = _rms(lat[:, :MLA_Q_RANK], qn_ref[...]).astype(_BF16)
    kvn = _rms(lat[:, MLA_Q_RANK:MLA_Q_RANK + MLA_KV_RANK], kn_ref[...]).astype(_BF16)
    kr = lat[:, MLA_Q_RANK + MLA_KV_RANK:].astype(_F32)
    cos = cos_ref[...]
    sin = sin_ref[...]
    lane = lax.broadcasted_iota(jnp.int32, cos.shape, 1)
    half = MLA_ROPE // 2
    first = lane < MLA_NOPE + half
    scale = (MLA_NOPE + MLA_ROPE) ** -0.5
    kpe = kr * cos + _rope_rotate(kr, half, first) * sin
    q_all = _dotf(qn, wq_ref[...])
    k_all = _dotf(kvn, wk_ref[...])
    v_all = _dotf(kvn, wv_ref[...])
    ones_hi = jnp.where(lane >= MLA_V, 1.0, 0.0)
    for h in range(MLA_HEADS):
        sl = slice(h * LANES, (h + 1) * LANES)
        qh = q_all[:, sl]
        qh = (qh * cos + _rope_rotate(qh, half, first) * sin) * scale
        q_ref[:, sl] = qh.astype(q_ref.dtype)
        k_ref[:, sl] = (k_all[:, sl] + kpe).astype(k_ref.dtype)
        v_ref[:, sl] = (v_all[:, sl] + ones_hi).astype(v_ref.dtype)


def _mla_prep(proj, col_block, qn, kn, wq, wk, wv, cos, sin, tm):
    B, S, _ = proj.shape
    wlat = MLA_Q_RANK + MLA_KV_RANK + LANES
    W = MLA_HEADS * LANES
    tm = min(tm, S)
    out = jax.ShapeDtypeStruct((B, S, W), _BF16)
    full = lambda shp: pl.BlockSpec(shp, lambda b, i: (0,) * len(shp))
    return pl.pallas_call(
        _mla_prep_kernel,
        out_shape=(out, out, out),
        grid=(B, S // tm),
        in_specs=[
            pl.BlockSpec((None, tm, wlat), lambda b, i: (b, i, col_block)),
            full((1, MLA_Q_RANK)), full((1, MLA_KV_RANK)),
            full(wq.shape), full(wk.shape), full(wv.shape),
            pl.BlockSpec((tm, LANES), lambda b, i: (i, 0)),
            pl.BlockSpec((tm, LANES), lambda b, i: (i, 0)),
        ],
        out_specs=(pl.BlockSpec((None, tm, W), lambda b, i: (b, i, 0)),) * 3,
        compiler_params=_cparams(("arbitrary", "arbitrary"), VMEM_LIMIT),
        name="mla_prep",
    )(proj, qn, kn, wq, wk, wv, cos, sin)


def _flash_kernel(q_ref, k_ref, v_ref, o_ref, s_ref, m_ref, acc_ref, *, tk):
    S = k_ref.shape[0]
    nk = S // tk
    q = q_ref[...]
    m_ref[...] = jnp.full(m_ref.shape, MASK_VALUE, _F32)
    acc_ref[...] = jnp.zeros(acc_ref.shape, _F32)

    def scores(i):
        return _dot_nt(q, k_ref[pl.ds(pl.multiple_of(i * tk, tk), tk), :])

    def softmax_update(i, slot):
        s = s_ref[slot]
        chunks = [s[:, c * LANES:(c + 1) * LANES] for c in range(tk // LANES)]
        mx = functools.reduce(jnp.maximum, chunks)
        m_old = m_ref[...]
        m_new = jnp.maximum(m_old, jnp.max(mx, axis=1, keepdims=True))
        p = jnp.concatenate([jnp.exp((c - m_new).astype(_BF16)) for c in chunks], axis=1)
        pv = _dotf(p, v_ref[pl.ds(pl.multiple_of(i * tk, tk), tk), :])
        acc_ref[...] = acc_ref[...] * jnp.exp(m_old - m_new) + pv
        m_ref[...] = m_new

    s_ref[0] = scores(0)

    def kv_pair(j, carry):
        i = 2 * j
        s_ref[1] = scores(i + 1)
        softmax_update(i, 0)
        s_ref[0] = scores(jnp.minimum(i + 2, nk - 1))
        softmax_update(i + 1, 1)
        return carry

    lax.fori_loop(0, nk // 2, kv_pair, 0)
    acc = acc_ref[...]
    lane = lax.broadcasted_iota(jnp.int32, acc.shape, 1)
    den = jnp.where(lane < MLA_V, pltpu.roll(acc, MLA_V, 1), acc)
    o_ref[...] = (acc / den).astype(o_ref.dtype)


def _mla_flash(q, k, v, tq, tk):
    B, S, W = q.shape
    tq = min(tq, S)
    tk = min(tk, S // 2)
    assert S % (2 * tk) == 0
    kern = functools.partial(_flash_kernel, tk=tk)
    return pl.pallas_call(
        kern,
        out_shape=jax.ShapeDtypeStruct((B, S, W), _BF16),
        grid=(B, MLA_HEADS, S // tq),
        in_specs=[
            pl.BlockSpec((None, tq, LANES), lambda b, h, i: (b, i, h)),
            pl.BlockSpec((None, S, LANES), lambda b, h, i: (b, 0, h)),
            pl.BlockSpec((None, S, LANES), lambda b, h, i: (b, 0, h)),
        ],
        out_specs=pl.BlockSpec((None, tq, LANES), lambda b, h, i: (b, i, h)),
        scratch_shapes=[pltpu.VMEM((2, tq, tk), _F32), pltpu.VMEM((tq, LANES), _F32),
                        pltpu.VMEM((tq, LANES), _F32)],
        compiler_params=_cparams(("arbitrary",) * 3, VMEM_LIMIT),
        name="mla_flash",
    )(q, k, v)


DIL_QBLK = 128
DIL_HALF = 64
DIL_KWIN = DIL_QBLK + 2 * DIL_HALF


def _dil_mask_table():
    qi = np.arange(2 * DIL_QBLK)[None, :, None] % DIL_QBLK
    ki = np.arange(DIL_KWIN)[None, None, :]
    back = (np.arange(3) * DIL_HALF)[:, None, None]
    valid = np.abs(ki - back - qi) <= DIL_HALF
    return jnp.asarray(np.where(valid, 0.0, MASK_VALUE), _F32)


def _dil_kernel(q_ref, k_ref, v_ref, mask_ref, o_ref, lse_ref, *, L):
    lane = lax.broadcasted_iota(jnp.int32, (DIL_QBLK, LANES), 1)
    low = lane < HEAD_DIM

    def blk_step(i, carry):
        q0 = pl.multiple_of(i * DIL_QBLK, DIL_QBLK)
        start = pl.multiple_of(jnp.clip(q0 - DIL_HALF, 0, L - DIL_KWIN), DIL_HALF)
        q = q_ref[pl.ds(q0, DIL_QBLK), :]
        kw = k_ref[pl.ds(start, DIL_KWIN), :]
        vw = v_ref[pl.ds(start, DIL_KWIN), :]
        zero = jnp.zeros_like(q)
        q2 = jnp.concatenate([jnp.where(low, q, zero), jnp.where(low, zero, q)], axis=0)
        q2 = q2 * jnp.asarray(HEAD_DIM ** -0.5, q.dtype)
        s = _dot_nt(q2, kw) + mask_ref[lax.div(q0 - start, DIL_HALF)]
        m = jnp.max(s, axis=1, keepdims=True)
        p = jnp.exp(s - m)
        l = jnp.sum(p, axis=1, keepdims=True)
        o2 = _dotf(p.astype(_BF16), vw) / l
        lse2 = m + jnp.log(l)
        o_ref[pl.ds(q0, DIL_QBLK), :] = jnp.where(low, o2[:DIL_QBLK], o2[DIL_QBLK:]).astype(o_ref.dtype)
        lse_ref[pl.ds(q0, DIL_QBLK), :] = jnp.where(low, lse2[:DIL_QBLK], lse2[DIL_QBLK:])
        return carry

    nblk = L // DIL_QBLK
    lax.fori_loop(0, nblk, blk_step, 0, unroll=4 if nblk % 4 == 0 else 2)


def _dil_attention(qkv):
    B, d, L, _ = qkv.shape
    assert L >= DIL_KWIN and L % DIL_QBLK == 0
    npair = DIL_HEADS // 2
    kern = functools.partial(_dil_kernel, L=L)
    return pl.pallas_call(
        kern,
        out_shape=(jax.ShapeDtypeStruct((B, d, L, DIL_WIDTH), _BF16),
                   jax.ShapeDtypeStruct((B, d, L, DIL_WIDTH), _F32)),
        grid=(B, d, npair),
        in_specs=[
            pl.BlockSpec((None, None, L, LANES), lambda b, r, p: (b, r, 0, p)),
            pl.BlockSpec((None, None, L, LANES), lambda b, r, p: (b, r, 0, npair + p)),
            pl.BlockSpec((None, None, L, LANES), lambda b, r, p: (b, r, 0, 2 * npair + p)),
            pl.BlockSpec((3, 2 * DIL_QBLK, DIL_KWIN), lambda b, r, p: (0, 0, 0)),
        ],
        out_specs=(pl.BlockSpec((None, None, L, LANES), lambda b, r, p: (b, r, 0, p)),) * 2,
        compiler_params=_cparams(("arbitrary",) * 3, VMEM_LIMIT),
        name="dil_attention",
    )(qkv, qkv, qkv, _dil_mask_table())


def _post_mixer(out, x_ref, gate_ref, g_ref, b_ref, shift2_ref, scale2_ref, wr_ref,
                xo_ref, h_ref, aff_ref):
    y = DEEPNORM_ALPHA * x_ref[...] + gate_ref[...] * out
    mu = jnp.mean(y, axis=-1, keepdims=True)
    yc = y - mu
    var = jnp.mean(yc * yc, axis=-1, keepdims=True)
    xn = yc * lax.rsqrt(var + LN_EPS) * g_ref[...] + b_ref[...]
    xo_ref[...] = xn
    h = xn * (1.0 + scale2_ref[...]) + shift2_ref[...]
    h_ref[...] = h
    hh, hl = _split2(h)
    wh, wl = _split2(wr_ref[...])
    logits = _dot_nt(wh, hh) + (_dot_nt(wh, hl) + _dot_nt(wl, hh))
    mx = jnp.max(logits, axis=0, keepdims=True)
    ex = jnp.exp(logits - mx)
    aff_ref[...] = ex / jnp.sum(ex, axis=0, keepdims=True)


def _outproj_ab_kernel(a1_ref, a2_ref, w1_ref, w2_ref, *rest):
    out = _dotf(a1_ref[...], w1_ref[...]) + _dotf(a2_ref[...], w2_ref[...])
    _post_mixer(out, *rest)


def _outproj_dil_kernel(o1, l1, o2, l2, o3, l3, w_ref, *rest):
    *rest, obuf, lbuf = rest
    nlt = obuf.shape[1]
    for g, (o_ref, l_ref) in enumerate(((o1, l1), (o2, l2), (o3, l3))):
        d, n, _ = o_ref.shape
        for r in range(d):
            rows = slice(None) if d == 1 else pl.ds(r, n, stride=d)
            o_r = o_ref[r].astype(_F32)
            l_r = l_ref[r]
            for c in range(nlt):
                obuf[g, c, rows, :] = o_r[:, c * LANES:(c + 1) * LANES]
                lbuf[g, c, rows, :] = l_r[:, c * LANES:(c + 1) * LANES]
    mixed = []
    for c in range(nlt):
        ls = [lbuf[g, c] for g in range(3)]
        mx = jnp.maximum(jnp.maximum(ls[0], ls[1]), ls[2])
        ws = [jnp.exp(l - mx) for l in ls]
        den = ws[0] + ws[1] + ws[2]
        mixed.append(sum((w / den) * obuf[g, c] for g, w in enumerate(ws)))
    out = _dotf(jnp.concatenate(mixed, axis=1).astype(_BF16), w_ref[...])
    _post_mixer(out, *rest)


def _outproj(kind, acts, weights, x, gate, ln_g, ln_b, shift2, scale2, wr_t, tm):
    B, S, D = x.shape
    E = wr_t.shape[0]
    tm = min(tm, S)
    nt = S // tm
    tok = lambda w: pl.BlockSpec((None, tm, w), lambda b, i: (b, i, 0))
    per_b = pl.BlockSpec((None, 1, D), lambda b, i: (b, 0, 0))
    full = lambda shp: pl.BlockSpec(shp, lambda b, i: (0,) * len(shp))
    if kind == "ab":
        kern, scratch = _outproj_ab_kernel, []
        in_specs = [tok(a.shape[-1]) for a in acts]
    else:
        kern = _outproj_dil_kernel
        in_specs = [pl.BlockSpec((None, a.shape[1], tm // a.shape[1], a.shape[3]), lambda b, i: (b, 0, i, 0))
                    for a in acts]
        scratch = [pltpu.VMEM((len(acts) // 2, acts[0].shape[3] // LANES, tm, LANES), _F32)] * 2
    in_specs += [full(w.shape) for w in weights]
    in_specs += [tok(D), per_b, full((1, D)), full((1, D)), per_b, per_b, full(wr_t.shape)]
    return pl.pallas_call(
        kern,
        out_shape=(jax.ShapeDtypeStruct((B, S, D), _F32),
                   jax.ShapeDtypeStruct((B, S, D), _F32),
                   jax.ShapeDtypeStruct((E, B * S), _F32)),
        grid=(B, nt),
        in_specs=in_specs,
        out_specs=(tok(D), tok(D), pl.BlockSpec((E, tm), lambda b, i: (0, b * nt + i))),
        scratch_shapes=scratch,
        compiler_params=_cparams(("arbitrary", "arbitrary"), VMEM_LIMIT),
        name="outproj_" + kind,
    )(*acts, *weights, x, gate, ln_g, ln_b, shift2, scale2, wr_t)


def _excl_cumsum_rows(x_bf, R):
    li = lax.broadcasted_iota(jnp.int32, (LANES, LANES), 0)
    lj = lax.broadcasted_iota(jnp.int32, (LANES, LANES), 1)
    upper = jnp.where(li < lj, 1.0, 0.0).astype(_BF16)
    ones = jnp.ones((LANES, LANES), _BF16)
    within = _dotf(x_bf, upper)
    rowtot = _dotf(x_bf, ones)
    ri = lax.broadcasted_iota(jnp.int32, (R, R), 0)
    rj = lax.broadcasted_iota(jnp.int32, (R, R), 1)
    lower = jnp.where(rj < ri, 1.0, 0.0).astype(_BF16)
    return within + _dotf(lower, rowtot.astype(_BF16))


def _select_kernel(aff_ref, cs_ref, cnt_ref, *, cap):
    aff = aff_ref[...]
    R = aff.shape[0]
    bits = pltpu.bitcast(aff, jnp.int32)
    capf = jnp.float32(cap)

    def count_ge(t):
        return jnp.sum(jnp.where(bits >= t, 1.0, 0.0))

    def search(_, lohi):
        lo, hi = lohi
        mid = lo + lax.shift_right_logical(hi - lo, 1)
        ok = count_ge(mid) >= capf
        return jnp.where(ok, mid, lo), jnp.where(ok, hi, mid)

    lo, _ = lax.fori_loop(0, 31, search, (jnp.int32(0), jnp.int32(0x7F800000)))
    gt = bits > lo
    eq = bits == lo
    need = capf - jnp.sum(jnp.where(gt, 1.0, 0.0))
    eq_rank = _excl_cumsum_rows(jnp.where(eq, 1.0, 0.0).astype(_BF16), R)
    sel = jnp.logical_or(gt, jnp.logical_and(eq, eq_rank < need))
    cs = _excl_cumsum_rows(jnp.where(sel, 1.0, 0.0).astype(_BF16), R)
    cs_ref[...] = jnp.where(sel, cs, -1.0)
    cnt_ref[...] = cs.astype(jnp.int32)


def _moe_select(aff_t, cap):
    E, N = aff_t.shape
    R = N // LANES
    kern = functools.partial(_select_kernel, cap=cap)
    spec = pl.BlockSpec((None, R, LANES), lambda e: (e, 0, 0))
    slot, before = pl.pallas_call(
        kern,
        out_shape=(jax.ShapeDtypeStruct((E, R, LANES), _F32), jax.ShapeDtypeStruct((E, R, LANES), jnp.int32)),
        grid=(E,),
        in_specs=[spec],
        out_specs=(spec, spec),
        compiler_params=_cparams(("arbitrary",), VMEM_LIMIT),
        name="moe_select",
    )(aff_t.reshape(E, R, LANES))
    return slot.reshape(E, N), before.reshape(E, N)


COMPACT_TILE = 256


def _compact_kernel(slot_ref, aff_ref, idx_ref, gate_ref):
    k = pl.program_id(1)
    slot = slot_ref[...]
    R = slot.shape[0]
    T = COMPACT_TILE
    m_bf = jnp.where(slot >= 0.0, 1.0, 0.0).astype(_BF16)
    ones8 = jnp.ones((8, LANES), _BF16)
    rowtot = _dot_nt(ones8, m_bf)
    ri = lax.broadcasted_iota(jnp.int32, (R, R), 0)
    rj = lax.broadcasted_iota(jnp.int32, (R, R), 1)
    incl = _dotf(rowtot.astype(_BF16), jnp.where(ri <= rj, 1.0, 0.0).astype(_BF16))
    excl = incl - rowtot
    c = (k * T + lax.broadcasted_iota(jnp.int32, (T, R), 0)).astype(_F32)
    oh = jnp.logical_and(excl[0:1, :] <= c, c < incl[0:1, :])
    ohb = jnp.where(oh, 1.0, 0.0).astype(_BF16)
    li = lax.broadcasted_iota(jnp.int32, (LANES, LANES), 0)
    lj = lax.broadcasted_iota(jnp.int32, (LANES, LANES), 1)
    loc_incl = _dotf(m_bf, jnp.where(li <= lj, 1.0, 0.0).astype(_BF16))
    a_hi, a_mid, a_lo = _split3(aff_ref[...])
    g_cs = _dotf(ohb, loc_incl.astype(_BF16))
    g_aff = _dotf(ohb, a_hi) + (_dotf(ohb, a_mid) + _dotf(ohb, a_lo))
    excl_sel = jnp.sum(jnp.where(oh, excl[0:1, :], 0.0), axis=1, keepdims=True)
    row_sel = jnp.sum(jnp.where(oh, rj[0:1, :].astype(_F32), 0.0), axis=1, keepdims=True)
    target = c[:, 0:1] - excl_sel + 1.0
    lane_sel = jnp.sum(jnp.where(g_cs < target, 1.0, 0.0), axis=1, keepdims=True)
    lane_f = lax.broadcasted_iota(jnp.int32, (T, LANES), 1).astype(_F32)
    gate = jnp.sum(jnp.where(lane_f == lane_sel, g_aff, 0.0), axis=1, keepdims=True)
    idx_ref[...] = jnp.broadcast_to(row_sel * LANES + lane_sel, (T, LANES)).astype(jnp.int32)
    gate_ref[...] = jnp.broadcast_to(gate, (T, LANES))


def _moe_compact(slot, aff_t, cap):
    E, N = slot.shape
    R = N // LANES
    T = COMPACT_TILE
    nt = cap // T
    spec_in = pl.BlockSpec((None, R, LANES), lambda e, k: (e, 0, 0))
    spec_out = pl.BlockSpec((T, LANES), lambda e, k: (e * nt + k, 0))
    return pl.pallas_call(
        _compact_kernel,
        out_shape=(jax.ShapeDtypeStruct((E * cap, LANES), jnp.int32),
                   jax.ShapeDtypeStruct((E * cap, LANES), _F32)),
        grid=(E, nt),
        in_specs=[spec_in, spec_in],
        out_specs=(spec_out, spec_out),
        compiler_params=_cparams(("arbitrary", "arbitrary"), VMEM_LIMIT),
        name="moe_compact",
    )(slot.reshape(E, R, LANES), aff_t.reshape(E, R, LANES))


def _row_copy(h_hbm, xbuf, sem, src_row, dst_row):
    return pltpu.make_async_copy(h_hbm.at[pl.ds(src_row, 1)], xbuf.at[pl.ds(dst_row, 1)], sem)


def _ffn_kernel(idx_ref, idx_next_ref, gate_ref, wg_ref, wu_ref, wd_ref, h_hbm, o_ref, xbuf, xb_ref, sems,
                *, tm, nsteps):
    step = pl.program_id(0)
    slot = lax.rem(step, 2)

    def drain(slot):
        def wait(i, carry):
            _row_copy(h_hbm, xbuf.at[slot], sems.at[slot], 0, i).wait()
            return carry

        lax.fori_loop(0, tm, wait, 0, unroll=8)

    @pl.when(step == 0)
    def _():
        def issue(i, carry):
            _row_copy(h_hbm, xbuf.at[0], sems.at[0], idx_ref[i], i).start()
            return carry

        lax.fori_loop(0, tm, issue, 0, unroll=8)

    drain(slot)
    xb_ref[...] = xbuf[slot].astype(_BF16)
    for i in range(tm):
        _row_copy(h_hbm, xbuf.at[1 - slot], sems.at[1 - slot], idx_next_ref[i], i).start()
    x = xb_ref[...]
    g = _dotf(x, wg_ref[...])
    u = _dotf(x, wu_ref[...])
    mid = (_silu(g) * u).astype(_BF16)
    y = _dotf(mid, wd_ref[...])
    gate = gate_ref[...]
    for j in range(y.shape[1] // LANES):
        o_ref[:, j * LANES:(j + 1) * LANES] = (y[:, j * LANES:(j + 1) * LANES] * gate).astype(o_ref.dtype)

    @pl.when(step == nsteps - 1)
    def _():
        drain(1 - slot)


def _moe_ffn(idx, gate, h, wg, wu, wd, cap, tm):
    E, D, F = wg.shape
    tm = min(tm, cap)
    nt = cap // tm
    nsteps = E * nt
    kern = functools.partial(_ffn_kernel, tm=tm, nsteps=nsteps)
    return pl.pallas_call(
        kern,
        out_shape=jax.ShapeDtypeStruct((E * cap, D), _BF16),
        grid=(nsteps,),
        in_specs=[
            pl.BlockSpec((tm,), lambda s: (s,), memory_space=pltpu.SMEM),
            pl.BlockSpec((tm,), lambda s: (jnp.minimum(s + 1, nsteps - 1),), memory_space=pltpu.SMEM),
            pl.BlockSpec((tm, LANES), lambda s: (s, 0)),
            pl.BlockSpec((None, D, F), lambda s: (s // nt, 0, 0)),
            pl.BlockSpec((None, D, F), lambda s: (s // nt, 0, 0)),
            pl.BlockSpec((None, F, D), lambda s: (s // nt, 0, 0)),
            pl.BlockSpec(memory_space=pl.ANY),
        ],
        out_specs=pl.BlockSpec((tm, D), lambda s: (s, 0)),
        scratch_shapes=[pltpu.VMEM((2, tm, D), _F32), pltpu.VMEM((tm, D), _BF16),
                        pltpu.SemaphoreType.DMA((2,))],
        compiler_params=_cparams(("arbitrary",), VMEM_LIMIT),
        name="moe_ffn",
    )(idx, idx, gate, wg, wu, wd, h)


COMBINE_TOK = 256
COMBINE_WIN = 128


def _combine_kernel(bstart_ref, slot_ref, x_ref, gate_ref, g_ref, b_ref, ye_hbm, o_ref,
                    ybuf, xtra, sems, xsem, acc_ref, *, cap, n_exp, nblk):
    blk = pl.program_id(0)
    par = lax.rem(blk, 2)
    W = COMBINE_WIN

    def aligned_start(b, e):
        s0 = bstart_ref[b * n_exp + e]
        return lax.shift_left(lax.shift_right_logical(s0, 4), 4)

    def window(b, e, j):
        row = e * cap + jnp.minimum(aligned_start(b, e) + j * W, cap - W)
        return pl.multiple_of(row, BF16_SUBLANES)

    def first_windows(b, par):
        return [pltpu.make_async_copy(ye_hbm.at[pl.ds(window(b, e, 0), W)],
                                      ybuf.at[par, pl.ds(e * W, W)], sems.at[par]) for e in range(n_exp)]

    @pl.when(blk == 0)
    def _():
        for cp in first_windows(0, 0):
            cp.start()

    @pl.when(blk + 1 < nblk)
    def _():
        for cp in first_windows(blk + 1, 1 - par):
            cp.start()

    for cp in first_windows(blk, par):
        cp.wait()

    slots = slot_ref[...]
    lane = lax.broadcasted_iota(jnp.int32, (COMBINE_TOK, W), 1)

    def onehot(e, row0, first_slot):
        slot_e = slots[:, e:e + 1].astype(jnp.int32)
        local = jnp.where(slot_e >= first_slot, slot_e - (row0 - e * cap), -1)
        return jnp.where(local == lane, 1.0, 0.0).astype(_BF16)

    placement = jnp.concatenate([onehot(e, window(blk, e, 0), 0) for e in range(n_exp)], axis=1)
    acc_ref[...] = _dotf(placement, ybuf[par])

    for e in range(n_exp):
        al = aligned_start(blk, e)
        s1 = bstart_ref[(blk + 1) * n_exp + e]
        nwin = lax.div(s1 - al + (W - 1), W)

        def extra(j, carry, e=e, al=al):
            r0 = window(blk, e, j)
            cp = pltpu.make_async_copy(ye_hbm.at[pl.ds(r0, W)], xtra, xsem)
            cp.start()
            cp.wait()
            acc_ref[...] += _dotf(onehot(e, r0, al + j * W), xtra[...])
            return carry

        lax.fori_loop(1, nwin, extra, 0)

    y = DEEPNORM_ALPHA * x_ref[...] + gate_ref[...] * acc_ref[...]
    mu = jnp.mean(y, axis=-1, keepdims=True)
    yc = y - mu
    var = jnp.mean(yc * yc, axis=-1, keepdims=True)
    o_ref[...] = yc * lax.rsqrt(var + LN_EPS) * g_ref[...] + b_ref[...]


def _moe_combine(bstart, slot_t, x, gate, ln_g, ln_b, ye, cap):
    B, S, D = x.shape
    N = B * S
    E = slot_t.shape[1]
    T = COMBINE_TOK
    per_seq = S // T
    kern = functools.partial(_combine_kernel, cap=cap, n_exp=E, nblk=N // T)
    out = pl.pallas_call(
        kern,
        out_shape=jax.ShapeDtypeStruct((N, D), _F32),
        grid_spec=pltpu.PrefetchScalarGridSpec(
            num_scalar_prefetch=1,
            grid=(N // T,),
            in_specs=[
                pl.BlockSpec((T, E), lambda i, bs: (i, 0)),
                pl.BlockSpec((T, D), lambda i, bs: (i, 0)),
                pl.BlockSpec((None, 1, D), lambda i, bs: (i // per_seq, 0, 0)),
                pl.BlockSpec((1, D), lambda i, bs: (0, 0)),
                pl.BlockSpec((1, D), lambda i, bs: (0, 0)),
                pl.BlockSpec(memory_space=pl.ANY),
            ],
            out_specs=pl.BlockSpec((T, D), lambda i, bs: (i, 0)),
            scratch_shapes=[
                pltpu.VMEM((2, E * COMBINE_WIN, D), _BF16),
                pltpu.VMEM((COMBINE_WIN, D), _BF16),
                pltpu.SemaphoreType.DMA((2,)),
                pltpu.SemaphoreType.DMA,
                pltpu.VMEM((T, D), _F32),
            ],
        ),
        compiler_params=_cparams(("arbitrary",), VMEM_LIMIT),
        name="moe_combine",
    )(bstart, slot_t, x.reshape(N, D), gate, ln_g, ln_b, ye)
    return out.reshape(B, S, D)


def _expert_choice(x1, h2, aff_t, gate, ln_g, ln_b, wg, wu, wd, ffn_tm):
    B, S, D = x1.shape
    N = B * S
    E = aff_t.shape[0]
    cap = EC_CAPACITY * N // E
    slot, before = _moe_select(aff_t, cap)
    idx_rep, gate_rep = _moe_compact(slot, aff_t, cap)
    ye = _moe_ffn(idx_rep[:, 0], gate_rep, h2.reshape(N, D), wg, wu, wd, cap, ffn_tm)
    bstart = jnp.concatenate([before[:, ::COMBINE_TOK], jnp.full((E, 1), cap, jnp.int32)], axis=1)
    return _moe_combine(bstart.T.reshape(-1), slot.T, x1, gate, ln_g, ln_b, ye, cap)


def _rope_np(n, dim):
    inv = 1.0 / (ROPE_THETA ** (jnp.arange(0, dim, 2, dtype=_F32) / dim))
    ang = jnp.arange(n, dtype=_F32)[:, None] * inv[None, :]
    return jnp.cos(ang), jnp.sin(ang)


def _head_rope_tables(S):
    cos, sin = _rope_np(S, HEAD_DIM)
    c = jnp.tile(cos, (1, LANES // (HEAD_DIM // 2)))
    s = jnp.tile(jnp.concatenate([-sin, sin], axis=1), (1, LANES // HEAD_DIM))
    return c, s


def _mla_rope_tables(S):
    cos, sin = _rope_np(S, MLA_ROPE)
    pad = LANES - MLA_NOPE - MLA_ROPE
    c = jnp.concatenate([jnp.ones((S, MLA_NOPE), _F32), cos, cos, jnp.zeros((S, pad), _F32)], axis=1)
    s = jnp.concatenate([jnp.zeros((S, MLA_NOPE), _F32), -sin, sin, jnp.zeros((S, pad), _F32)], axis=1)
    return c, s


def _prep_ab_weights(w_in, w_q_up, w_kv_up, w_out):
    D = w_in.shape[0]
    s1 = 3 * NA_WIDTH
    s3 = s1 + MLA_Q_RANK + MLA_KV_RANK
    pad = LANES - MLA_NOPE - MLA_ROPE
    krope = jnp.concatenate([jnp.zeros((D, MLA_NOPE), _F32), w_in[:, s3:], jnp.zeros((D, pad), _F32)], axis=1)
    w_in_p = jnp.concatenate([w_in[:, :s3], krope], axis=1).astype(_BF16)
    wq = w_q_up.reshape(MLA_Q_RANK, MLA_HEADS, MLA_NOPE + MLA_ROPE)
    wq = jnp.pad(wq, ((0, 0), (0, 0), (0, pad))).reshape(MLA_Q_RANK, MLA_HEADS * LANES).astype(_BF16)
    wkv = w_kv_up.reshape(MLA_KV_RANK, MLA_HEADS, MLA_NOPE + MLA_V)
    wk = jnp.pad(wkv[:, :, :MLA_NOPE], ((0, 0), (0, 0), (0, LANES - MLA_NOPE)))
    wv = jnp.pad(wkv[:, :, MLA_NOPE:], ((0, 0), (0, 0), (0, LANES - MLA_V)))
    wk = wk.reshape(MLA_KV_RANK, MLA_HEADS * LANES).astype(_BF16)
    wv = wv.reshape(MLA_KV_RANK, MLA_HEADS * LANES).astype(_BF16)
    w_out_na = w_out[:NA_WIDTH].astype(_BF16)
    wo = w_out[NA_WIDTH:].reshape(MLA_HEADS, MLA_V, D)
    w_out_mla = jnp.pad(wo, ((0, 0), (0, LANES - MLA_V), (0, 0))).reshape(MLA_HEADS * LANES, D).astype(_BF16)
    return w_in_p, wq, wk, wv, w_out_na, w_out_mla


def _trunk(x, c_mods, P, layer_w):
    B, S, D = x.shape
    cos_h, sin_h = _head_rope_tables(S)
    cos_r, sin_r = _mla_rope_tables(S)
    for layer in range(DEPTH):
        j = layer // 2
        mm = c_mods[layer]
        lw = layer_w[layer]
        if layer % 2 == 0:
            proj = _inproj(x, mm["mix_shift"], mm["mix_scale"], lw["w_in"], cos_h, sin_h,
                           dil=1, rope_cols=0, half=1, tm=512).reshape(B, S, -1)
            na = _na_attention(proj, lw["bias_tab"])
            wlat = MLA_Q_RANK + MLA_KV_RANK + LANES
            q, k, v = _mla_prep(proj, (3 * NA_WIDTH) // wlat, lw["q_norm"], lw["kv_norm"],
                                lw["wq"], lw["wk"], lw["wv"], cos_r, sin_r, tm=512)
            mla = _mla_flash(q, k, v, tq=256, tk=512)
            acts, weights, kind = [na, mla], [lw["w_out_na"], lw["w_out_mla"]], "ab"
        else:
            acts = []
            for g, (window, dil) in enumerate(DIL_PAIRS):
                assert (window // 2) // dil == DIL_HALF
                qkv = _inproj(x, mm["mix_shift"], mm["mix_scale"], lw["w_in"][g], cos_h, sin_h,
                              dil=dil, rope_cols=2 * DIL_WIDTH, half=HEAD_DIM // 2, tm=512)
                o, lse = _dil_attention(qkv)
                acts += [o, lse]
            weights, kind = [lw["w_out"]], "dil"
        x1, h2, aff_t = _outproj(kind, acts, weights, x, mm["mix_gate"], lw["mix_ln_g"], lw["mix_ln_b"],
                                 mm["moe_shift"], mm["moe_scale"], lw["wr_t"], tm=256)
        x = _expert_choice(x1, h2, aff_t, mm["moe_gate"], lw["moe_ln_g"], lw["moe_ln_b"],
                           lw["wg"], lw["wu"], lw["wd"], ffn_tm=256)
    return x


def kernel(x_prompt, x_sample, c_prompt, c_sample, ab_w_in, na_rpb, mla_q_norm, mla_w_q_up, mla_kv_norm,
           mla_w_kv_up, ab_w_out, dil_w_in, dil_w_out, mix_ada_w, mix_ada_b, mix_ln_g, mix_ln_b,
           moe_ada_w, moe_ada_b, moe_ln_g, moe_ln_b, moe_w_router, moe_w_gate, moe_w_up, moe_w_down):
    D = x_prompt.shape[-1]
    nb_p, nb_s = c_prompt.shape[0], c_sample.shape[0]
    bp = -(-(nb_p + nb_s) // 8) * 8
    c_all = jnp.zeros((bp, D), _F32).at[:nb_p].set(c_prompt).at[nb_p:nb_p + nb_s].set(c_sample)
    mix_mod = _ada_mod(c_all, mix_ada_w, mix_ada_b)
    moe_mod = _ada_mod(c_all, moe_ada_w, moe_ada_b)

    def mods_for(lo, n):
        out = []
        for layer in range(DEPTH):
            d = {}
            for name, mod in (("mix", mix_mod), ("moe", moe_mod)):
                m = mod[layer, lo:lo + n]
                d[name + "_shift"] = m[:, None, :D]
                d[name + "_scale"] = m[:, None, D:2 * D]
                d[name + "_gate"] = m[:, None, 2 * D:]
            out.append(d)
        return out

    layer_w = []
    for layer in range(DEPTH):
        j = layer // 2
        lw = {
            "mix_ln_g": mix_ln_g[layer][None], "mix_ln_b": mix_ln_b[layer][None],
            "moe_ln_g": moe_ln_g[layer][None], "moe_ln_b": moe_ln_b[layer][None],
            "wr_t": moe_w_router[layer].T,
            "wg": moe_w_gate[layer].astype(_BF16), "wu": moe_w_up[layer].astype(_BF16),
            "wd": moe_w_down[layer].astype(_BF16),
        }
        if layer % 2 == 0:
            w_in_p, wq, wk, wv, w_out_na, w_out_mla = _prep_ab_weights(
                ab_w_in[j], mla_w_q_up[j], mla_w_kv_up[j], ab_w_out[j])
            lw.update(w_in=w_in_p, wq=wq, wk=wk, wv=wv, w_out_na=w_out_na, w_out_mla=w_out_mla,
                      q_norm=mla_q_norm[j][None], kv_norm=mla_kv_norm[j][None],
                      bias_tab=_na_bias_table(na_rpb[j]))
        else:
            wi = dil_w_in[j].reshape(D, len(DIL_PAIRS), 3 * DIL_WIDTH).astype(_BF16)
            lw.update(w_in=[wi[:, g] for g in range(len(DIL_PAIRS))], w_out=dil_w_out[j].astype(_BF16))
        layer_w.append(lw)

    y_prompt = _trunk(x_prompt, mods_for(0, nb_p), None, layer_w)
    y_sample = _trunk(x_sample, mods_for(nb_p, nb_s), None, layer_w)
    return (y_prompt, y_sample)
```

```python
import functools

import numpy as np
import jax
import jax.numpy as jnp
from jax import lax
from jax.experimental import pallas as pl
from jax.experimental.pallas import tpu as pltpu

GRID_W = 64
HEAD_DIM = 64
NA_HEADS = 8
NA_WIN_H = 8
NA_WIN_W = 16
MLA_HEADS = 8
MLA_Q_RANK = 256
MLA_KV_RANK = 128
MLA_NOPE = 64
MLA_ROPE = 32
MLA_V = 64
DIL_PAIRS = ((128, 1), (512, 4), (2048, 16))
DIL_HEADS = 8
N_EXPERTS = 16
EC_CAPACITY = 2
ROPE_THETA = 10000.0
LN_EPS = 1e-5
RMS_EPS = 1e-6
DEPTH = 2
DEEPNORM_ALPHA = (2 * DEPTH) ** 0.25
NA_WIDTH = NA_HEADS * HEAD_DIM
DIL_WIDTH = DIL_HEADS * HEAD_DIM

LANES = 128
BF16_SUBLANES = 16
MASK_VALUE = -1e30
VMEM_LIMIT = 56 * 1024 * 1024

_F32 = jnp.float32
_BF16 = jnp.bfloat16


def _cparams(sem, vmem=None):
    return pltpu.CompilerParams(dimension_semantics=sem, vmem_limit_bytes=vmem)


def _dotf(a, b):
    return jnp.dot(a, b, preferred_element_type=_F32)


def _dot_nt(a, b):
    return lax.dot_general(a, b, (((1,), (1,)), ((), ())), preferred_element_type=_F32)


def _split2(x):
    hi = x.astype(_BF16)
    lo = (x - hi.astype(_F32)).astype(_BF16)
    return hi, lo


def _split3(x):
    hi = x.astype(_BF16)
    r = x - hi.astype(_F32)
    mid = r.astype(_BF16)
    lo = (r - mid.astype(_F32)).astype(_BF16)
    return hi, mid, lo


def _silu(x):
    return x / (1.0 + jnp.exp(-x))


def _rope_rotate(x, half, first_half):
    n = x.shape[-1]
    fwd = pltpu.roll(x, n - half, 1)
    bwd = pltpu.roll(x, half, 1)
    return jnp.where(first_half, fwd, bwd)


def _ada_kernel(c_ref, w_ref, b_ref, o_ref):
    a = _silu(c_ref[...])
    ah, al = _split2(a)
    wh, wl = _split2(w_ref[...])
    o_ref[...] = _dotf(ah, wh) + (_dotf(ah, wl) + _dotf(al, wh)) + b_ref[...]


def _ada_mod(c, w, b):
    L, D, N3 = w.shape
    Bp = c.shape[0]
    tn = min(512, N3)
    return pl.pallas_call(
        _ada_kernel,
        out_shape=jax.ShapeDtypeStruct((L, Bp, N3), _F32),
        grid=(L, N3 // tn),
        in_specs=[
            pl.BlockSpec((Bp, D), lambda l, j: (0, 0)),
            pl.BlockSpec((None, D, tn), lambda l, j: (l, 0, j)),
            pl.BlockSpec((None, 1, tn), lambda l, j: (l, 0, j)),
        ],
        out_specs=pl.BlockSpec((None, Bp, tn), lambda l, j: (l, 0, j)),
        compiler_params=_cparams(("arbitrary", "arbitrary")),
        name="ada_mod",
    )(c, w, b.reshape(L, 1, N3))


INPROJ_CHUNK = 2048
INPROJ_ROWS = 512


def _inproj_kernel(x_ref, shift_ref, scale_ref, w_ref, cos_ref, sin_ref, o_ref, *scratch,
                   dil, tm, rs, rope_cols, half):
    if dil == 1:
        x = x_ref[...]
        cos, sin = cos_ref[...], sin_ref[...]
    else:
        (xs_ref,) = scratch
        nlt = xs_ref.shape[0]

        @pl.when(pl.program_id(2) == 0)
        def _():
            for c in range(nlt):
                xs_ref[c] = x_ref[:, c * LANES:(c + 1) * LANES]

        strides = [pl.ds(pl.program_id(2) * rs + q, tm, stride=dil) for q in range(rs)]
        x = jnp.concatenate(
            [jnp.concatenate([xs_ref[c, rows, :] for c in range(nlt)], axis=1) for rows in strides], axis=0)
        cos = jnp.concatenate([cos_ref[rows, :] for rows in strides], axis=0)
        sin = jnp.concatenate([sin_ref[rows, :] for rows in strides], axis=0)
    h = x * (1.0 + scale_ref[...]) + shift_ref[...]
    acc = _dotf(h.astype(_BF16), w_ref[...])
    n = acc.shape[1]

    def store(cols, blk):
        blk = blk.astype(o_ref.dtype)
        for q in range(rs):
            o_ref[q, :, cols] = blk[q * tm:(q + 1) * tm]

    if rope_cols:
        lane = lax.broadcasted_iota(jnp.int32, cos.shape, 1)
        first = (lane % (2 * half)) < half
        for j in range(rope_cols // LANES):
            blk = acc[:, j * LANES:(j + 1) * LANES]
            store(slice(j * LANES, (j + 1) * LANES), blk * cos + _rope_rotate(blk, half, first) * sin)
    store(slice(rope_cols, n), acc[:, rope_cols:])


def _inproj(x, shift, scale, w, cos, sin, *, dil, rope_cols, half, tm):
    B, S, D = x.shape
    N = w.shape[1]
    L = S // dil
    tm = min(tm, L, INPROJ_CHUNK // dil)
    chunk = tm * dil
    rs = max(1, min(dil, INPROJ_ROWS // tm))
    kern = functools.partial(_inproj_kernel, dil=dil, tm=tm, rs=rs, rope_cols=rope_cols, half=half)
    return pl.pallas_call(
        kern,
        out_shape=jax.ShapeDtypeStruct((B, dil, L, N), _BF16),
        grid=(B, L // tm, dil // rs),
        in_specs=[
            pl.BlockSpec((None, chunk, D), lambda b, j, r: (b, j, 0)),
            pl.BlockSpec((None, 1, D), lambda b, j, r: (b, 0, 0)),
            pl.BlockSpec((None, 1, D), lambda b, j, r: (b, 0, 0)),
            pl.BlockSpec((D, N), lambda b, j, r: (0, 0)),
            pl.BlockSpec((chunk, LANES), lambda b, j, r: (j, 0)),
            pl.BlockSpec((chunk, LANES), lambda b, j, r: (j, 0)),
        ],
        out_specs=pl.BlockSpec((None, rs, tm, N), lambda b, j, r: (b, r, j, 0)),
        scratch_shapes=[pltpu.VMEM((D // LANES, chunk, LANES), _F32)] if dil > 1 else [],
        compiler_params=_cparams(("arbitrary",) * 3, VMEM_LIMIT),
        name="inproj",
    )(x, shift, scale, w, cos, sin)


def _na_kernel(q_ref, k_ref, v_ref, bias_ref, o_ref, *, rows):
    nk = NA_WIN_H * GRID_W
    lane = lax.broadcasted_iota(jnp.int32, (GRID_W, LANES), 1)
    low = lane < HEAD_DIM

    def row_step(r, carry):
        r0 = jnp.clip(r - NA_WIN_H // 2, 0, rows - NA_WIN_H)
        case = r - r0
        q = q_ref[pl.ds(pl.multiple_of(r * GRID_W, GRID_W), GRID_W), :]
        kw = k_ref[pl.ds(pl.multiple_of(r0 * GRID_W, GRID_W), nk), :]
        vw = v_ref[pl.ds(pl.multiple_of(r0 * GRID_W, GRID_W), nk), :]
        zero = jnp.zeros_like(q)
        q2 = jnp.concatenate([jnp.where(low, q, zero), jnp.where(low, zero, q)], axis=0)
        q2 = q2 * jnp.asarray(HEAD_DIM ** -0.5, q.dtype)
        s = _dot_nt(q2, kw) + bias_ref[:, case].reshape(2 * GRID_W, nk)
        m = jnp.max(s, axis=1, keepdims=True)
        p = jnp.exp(s - m)
        l = jnp.sum(p, axis=1, keepdims=True)
        o2 = _dotf(p.astype(_BF16), vw) / l
        o = jnp.where(low, o2[:GRID_W], o2[GRID_W:])
        o_ref[pl.ds(pl.multiple_of(r * GRID_W, GRID_W), GRID_W), :] = o.astype(o_ref.dtype)
        return carry

    lax.fori_loop(0, rows, row_step, 0, unroll=4)


def _na_bias_table(rpb):
    kh, kw = NA_WIN_H, NA_WIN_W
    H = rpb.shape[0]
    W = GRID_W
    P = 2 * W
    lead = (W - 1) - (kw - 1)
    row = jnp.pad(rpb.astype(_F32), ((0, 0), (0, 0), (lead, P - lead - (2 * kw - 1))))
    flat = jnp.tile(row, (1, 1, W))[:, :, :W * (P - 1)]
    toep = flat.reshape(H, 2 * kh - 1, W, P - 1)[:, :, :, W - 1:]
    cols = np.arange(W)
    c0 = np.clip(cols - kw // 2, 0, W - kw)
    valid = (cols[None, :] >= c0[:, None]) & (cols[None, :] < c0[:, None] + kw)
    toep = jnp.where(valid[None, None], toep, MASK_VALUE)
    per_case = [jnp.transpose(toep[:, kh - 1 - case:2 * kh - 1 - case], (0, 2, 1, 3)) for case in range(kh)]
    return jnp.stack(per_case, axis=1).reshape(H, kh, W, kh * W)


def _na_attention(proj, bias_tab):
    B, S, _ = proj.shape
    rows = S // GRID_W
    assert rows >= NA_WIN_H
    npair = NA_HEADS // 2
    kern = functools.partial(_na_kernel, rows=rows)
    return pl.pallas_call(
        kern,
        out_shape=jax.ShapeDtypeStruct((B, S, NA_WIDTH), _BF16),
        grid=(B, npair),
        in_specs=[
            pl.BlockSpec((None, S, LANES), lambda b, p: (b, 0, p)),
            pl.BlockSpec((None, S, LANES), lambda b, p: (b, 0, npair + p)),
            pl.BlockSpec((None, S, LANES), lambda b, p: (b, 0, 2 * npair + p)),
            pl.BlockSpec((2, NA_WIN_H, GRID_W, NA_WIN_H * GRID_W), lambda b, p: (p, 0, 0, 0)),
        ],
        out_specs=pl.BlockSpec((None, S, LANES), lambda b, p: (b, 0, p)),
        compiler_params=_cparams(("arbitrary", "arbitrary"), VMEM_LIMIT),
        name="na_attention",
    )(proj, proj, proj, bias_tab)


def _rms(x, g):
    xf = x.astype(_F32)
    return xf * lax.rsqrt(jnp.mean(xf * xf, axis=-1, keepdims=True) + RMS_EPS) * g


def _mla_prep_kernel(lat_ref, qn_ref, kn_ref, wq_ref, wk_ref, wv_ref, cos_ref, sin_ref,
                     q_ref, k_ref, v_ref):
    lat = lat_ref[...]
    qn = _rms(lat[:, :MLA_Q_RANK], qn_ref[...]).astype(_BF16)
    kvn = _rms(lat[:, MLA_Q_RANK:MLA_Q_RANK + MLA_KV_RANK], kn_ref[...]).astype(_BF16)
    kr = lat[:, MLA_Q_RANK + MLA_KV_RANK:].astype(_F32)
    cos = cos_ref[...]
    sin = sin_ref[...]
    lane = lax.broadcasted_iota(jnp.int32, cos.shape, 1)
    half = MLA_ROPE // 2
    first = lane < MLA_NOPE + half
    scale = (MLA_NOPE + MLA_ROPE) ** -0.5
    kpe = kr * cos + _rope_rotate(kr, half, first) * sin
    q_all = _dotf(qn, wq_ref[...])
    k_all = _dotf(kvn, wk_ref[...])
    v_all = _dotf(kvn, wv_ref[...])
    ones_hi = jnp.where(lane >= MLA_V, 1.0, 0.0)
    for h in range(MLA_HEADS):
        sl = slice(h * LANES, (h + 1) * LANES)
        qh = q_all[:, sl]
        qh = (qh * cos + _rope_rotate(qh, half, first) * sin) * scale
        q_ref[:, sl] = qh.astype(q_ref.dtype)
        k_ref[:, sl] = (k_all[:, sl] + kpe).astype(k_ref.dtype)
        v_ref[:, sl] = (v_all[:, sl] + ones_hi).astype(v_ref.dtype)


def _mla_prep(proj, col_block, qn, kn, wq, wk, wv, cos, sin, tm):
    B, S, _ = proj.shape
    wlat = MLA_Q_RANK + MLA_KV_RANK + LANES
    W = MLA_HEADS * LANES
    tm = min(tm, S)
    out = jax.ShapeDtypeStruct((B, S, W), _BF16)
    full = lambda shp: pl.BlockSpec(shp, lambda b, i: (0,) * len(shp))
    return pl.pallas_call(
        _mla_prep_kernel,
        out_shape=(out, out, out),
        grid=(B, S // tm),
        in_specs=[
            pl.BlockSpec((None, tm, wlat), lambda b, i: (b, i, col_block)),
            full((1, MLA_Q_RANK)), full((1, MLA_KV_RANK)),
            full(wq.shape), full(wk.shape), full(wv.shape),
            pl.BlockSpec((tm, LANES), lambda b, i: (i, 0)),
            pl.BlockSpec((tm, LANES), lambda b, i: (i, 0)),
        ],
        out_specs=(pl.BlockSpec((None, tm, W), lambda b, i: (b, i, 0)),) * 3,
        compiler_params=_cparams(("arbitrary", "arbitrary"), VMEM_LIMIT),
        name="mla_prep",
    )(proj, qn, kn, wq, wk, wv, cos, sin)


def _flash_kernel(q_ref, k_ref, v_ref, o_ref, s_ref, m_ref, acc_ref, *, tq, tk):
    S = k_ref.shape[0]
    npair = S // (2 * tk)
    ntile = S // tq

    def reset():
        m_ref[...] = jnp.full(m_ref.shape, MASK_VALUE, _F32)
        acc_ref[...] = jnp.zeros(acc_ref.shape, _F32)

    def scores(tile, i):
        q = q_ref[pl.ds(pl.multiple_of(tile * tq, tq), tq), :]
        return _dot_nt(q, k_ref[pl.ds(pl.multiple_of(i * tk, tk), tk), :])

    def softmax_update(i, slot):
        s = s_ref[slot]
        chunks = [s[:, c * LANES:(c + 1) * LANES] for c in range(tk // LANES)]
        mx = functools.reduce(jnp.maximum, chunks)
        m_old = m_ref[...]
        m_new = jnp.maximum(m_old, jnp.max(mx, axis=1, keepdims=True))
        p = jnp.concatenate([jnp.exp((c - m_new).astype(_BF16)) for c in chunks], axis=1)
        pv = _dotf(p, v_ref[pl.ds(pl.multiple_of(i * tk, tk), tk), :])
        acc_ref[...] = acc_ref[...] * jnp.exp(m_old - m_new) + pv
        m_ref[...] = m_new

    reset()
    s_ref[0] = scores(0, 0)

    def kv_pair(t, carry):
        tile = lax.div(t, npair)
        i = 2 * lax.rem(t, npair)
        last = i == 2 * (npair - 1)
        s_ref[1] = scores(tile, i + 1)
        softmax_update(i, 0)
        s_ref[0] = scores(jnp.where(last, jnp.minimum(tile + 1, ntile - 1), tile), jnp.where(last, 0, i + 2))
        softmax_update(i + 1, 1)

        @pl.when(last)
        def _():
            acc = acc_ref[...]
            lane = lax.broadcasted_iota(jnp.int32, acc.shape, 1)
            den = jnp.where(lane < MLA_V, pltpu.roll(acc, MLA_V, 1), acc)
            o_ref[pl.ds(pl.multiple_of(tile * tq, tq), tq), :] = (acc / den).astype(o_ref.dtype)
            reset()

        return carry

    lax.fori_loop(0, ntile * npair, kv_pair, 0)


def _mla_flash(q, k, v, tq, tk):
    B, S, W = q.shape
    tq = min(tq, S)
    tk = min(tk, S // 2)
    assert S % (2 * tk) == 0 and S % tq == 0
    kern = functools.partial(_flash_kernel, tq=tq, tk=tk)
    spec = lambda: pl.BlockSpec((None, S, LANES), lambda b, h: (b, 0, h))
    return pl.pallas_call(
        kern,
        out_shape=jax.ShapeDtypeStruct((B, S, W), _BF16),
        grid=(B, MLA_HEADS),
        in_specs=[spec(), spec(), spec()],
        out_specs=spec(),
        scratch_shapes=[pltpu.VMEM((2, tq, tk), _F32), pltpu.VMEM((tq, LANES), _F32),
                        pltpu.VMEM((tq, LANES), _F32)],
        compiler_params=_cparams(("arbitrary",) * 2, VMEM_LIMIT),
        name="mla_flash",
    )(q, k, v)


DIL_QBLK = 128
DIL_HALF = 64
DIL_KWIN = DIL_QBLK + 2 * DIL_HALF


def _dil_mask_table():
    qi = np.arange(2 * DIL_QBLK)[None, :, None] % DIL_QBLK
    ki = np.arange(DIL_KWIN)[None, None, :]
    back = (np.arange(3) * DIL_HALF)[:, None, None]
    valid = np.abs(ki - back - qi) <= DIL_HALF
    return jnp.asarray(np.where(valid, 0.0, MASK_VALUE), _F32)


def _dil_kernel(q_ref, k_ref, v_ref, mask_ref, o_ref, lse_ref, *, L):
    lane = lax.broadcasted_iota(jnp.int32, (DIL_QBLK, LANES), 1)
    low = lane < HEAD_DIM

    def blk_step(i, carry):
        q0 = pl.multiple_of(i * DIL_QBLK, DIL_QBLK)
        start = pl.multiple_of(jnp.clip(q0 - DIL_HALF, 0, L - DIL_KWIN), DIL_HALF)
        q = q_ref[pl.ds(q0, DIL_QBLK), :]
        kw = k_ref[pl.ds(start, DIL_KWIN), :]
        vw = v_ref[pl.ds(start, DIL_KWIN), :]
        zero = jnp.zeros_like(q)
        q2 = jnp.concatenate([jnp.where(low, q, zero), jnp.where(low, zero, q)], axis=0)
        q2 = q2 * jnp.asarray(HEAD_DIM ** -0.5, q.dtype)
        s = _dot_nt(q2, kw) + mask_ref[lax.div(q0 - start, DIL_HALF)]
        m = jnp.max(s, axis=1, keepdims=True)
        p = jnp.exp(s - m)
        l = jnp.sum(p, axis=1, keepdims=True)
        o2 = _dotf(p.astype(_BF16), vw) / l
        lse2 = m + jnp.log(l)
        o_ref[pl.ds(q0, DIL_QBLK), :] = jnp.where(low, o2[:DIL_QBLK], o2[DIL_QBLK:]).astype(o_ref.dtype)
        lse_ref[pl.ds(q0, DIL_QBLK), :] = jnp.where(low, lse2[:DIL_QBLK], lse2[DIL_QBLK:])
        return carry

    nblk = L // DIL_QBLK
    lax.fori_loop(0, nblk, blk_step, 0, unroll=4 if nblk % 4 == 0 else 2)


def _dil_attention(qkv):
    B, d, L, _ = qkv.shape
    assert L >= DIL_KWIN and L % DIL_QBLK == 0
    npair = DIL_HEADS // 2
    kern = functools.partial(_dil_kernel, L=L)
    return pl.pallas_call(
        kern,
        out_shape=(jax.ShapeDtypeStruct((B, d, L, DIL_WIDTH), _BF16),
                   jax.ShapeDtypeStruct((B, d, L, DIL_WIDTH), _F32)),
        grid=(B, d, npair),
        in_specs=[
            pl.BlockSpec((None, None, L, LANES), lambda b, r, p: (b, r, 0, p)),
            pl.BlockSpec((None, None, L, LANES), lambda b, r, p: (b, r, 0, npair + p)),
            pl.BlockSpec((None, None, L, LANES), lambda b, r, p: (b, r, 0, 2 * npair + p)),
            pl.BlockSpec((3, 2 * DIL_QBLK, DIL_KWIN), lambda b, r, p: (0, 0, 0)),
        ],
        out_specs=(pl.BlockSpec((None, None, L, LANES), lambda b, r, p: (b, r, 0, p)),) * 2,
        compiler_params=_cparams(("arbitrary",) * 3, VMEM_LIMIT),
        name="dil_attention",
    )(qkv, qkv, qkv, _dil_mask_table())


def _post_mixer(out, x_ref, gate_ref, g_ref, b_ref, shift2_ref, scale2_ref, wr_ref,
                xo_ref, h_ref, aff_ref):
    y = DEEPNORM_ALPHA * x_ref[...] + gate_ref[...] * out
    mu = jnp.mean(y, axis=-1, keepdims=True)
    yc = y - mu
    var = jnp.mean(yc * yc, axis=-1, keepdims=True)
    xn = yc * lax.rsqrt(var + LN_EPS) * g_ref[...] + b_ref[...]
    xo_ref[...] = xn
    h = xn * (1.0 + scale2_ref[...]) + shift2_ref[...]
    h_ref[...] = h
    hh, hl = _split2(h)
    wh, wl = _split2(wr_ref[...])
    logits = _dot_nt(wh, hh) + (_dot_nt(wh, hl) + _dot_nt(wl, hh))
    mx = jnp.max(logits, axis=0, keepdims=True)
    ex = jnp.exp(logits - mx)
    aff_ref[...] = ex / jnp.sum(ex, axis=0, keepdims=True)


def _outproj_ab_kernel(a1_ref, a2_ref, w1_ref, w2_ref, *rest):
    out = _dotf(a1_ref[...], w1_ref[...]) + _dotf(a2_ref[...], w2_ref[...])
    _post_mixer(out, *rest)


def _outproj_dil_kernel(o1, l1, o2, l2, o3, l3, w_ref, *rest):
    *rest, obuf, lbuf = rest
    nlt = obuf.shape[1]
    for g, (o_ref, l_ref) in enumerate(((o1, l1), (o2, l2), (o3, l3))):
        d, n, _ = o_ref.shape
        for r in range(d):
            rows = slice(None) if d == 1 else pl.ds(r, n, stride=d)
            o_r = o_ref[r].astype(_F32)
            l_r = l_ref[r]
            for c in range(nlt):
                obuf[g, c, rows, :] = o_r[:, c * LANES:(c + 1) * LANES]
                lbuf[g, c, rows, :] = l_r[:, c * LANES:(c + 1) * LANES]
    mixed = []
    for c in range(nlt):
        ls = [lbuf[g, c] for g in range(3)]
        mx = jnp.maximum(jnp.maximum(ls[0], ls[1]), ls[2])
        ws = [jnp.exp(l - mx) for l in ls]
        den = ws[0] + ws[1] + ws[2]
        mixed.append(sum((w / den) * obuf[g, c] for g, w in enumerate(ws)))
    out = _dotf(jnp.concatenate(mixed, axis=1).astype(_BF16), w_ref[...])
    _post_mixer(out, *rest)


def _outproj(kind, acts, weights, x, gate, ln_g, ln_b, shift2, scale2, wr_t, tm):
    B, S, D = x.shape
    E = wr_t.shape[0]
    tm = min(tm, S)
    nt = S // tm
    tok = lambda w: pl.BlockSpec((None, tm, w), lambda b, i: (b, i, 0))
    per_b = pl.BlockSpec((None, 1, D), lambda b, i: (b, 0, 0))
    full = lambda shp: pl.BlockSpec(shp, lambda b, i: (0,) * len(shp))
    if kind == "ab":
        kern, scratch = _outproj_ab_kernel, []
        in_specs = [tok(a.shape[-1]) for a in acts]
    else:
        kern = _outproj_dil_kernel
        in_specs = [pl.BlockSpec((None, a.shape[1], tm // a.shape[1], a.shape[3]), lambda b, i: (b, 0, i, 0))
                    for a in acts]
        scratch = [pltpu.VMEM((len(acts) // 2, acts[0].shape[3] // LANES, tm, LANES), _F32)] * 2
    in_specs += [full(w.shape) for w in weights]
    in_specs += [tok(D), per_b, full((1, D)), full((1, D)), per_b, per_b, full(wr_t.shape)]
    return pl.pallas_call(
        kern,
        out_shape=(jax.ShapeDtypeStruct((B, S, D), _F32),
                   jax.ShapeDtypeStruct((B, S, D), _F32),
                   jax.ShapeDtypeStruct((E, B * S), _F32)),
        grid=(B, nt),
        in_specs=in_specs,
        out_specs=(tok(D), tok(D), pl.BlockSpec((E, tm), lambda b, i: (0, b * nt + i))),
        scratch_shapes=scratch,
        compiler_params=_cparams(("arbitrary", "arbitrary"), VMEM_LIMIT),
        name="outproj_" + kind,
    )(*acts, *weights, x, gate, ln_g, ln_b, shift2, scale2, wr_t)


def _excl_cumsum_rows(x_bf, R):
    li = lax.broadcasted_iota(jnp.int32, (LANES, LANES), 0)
    lj = lax.broadcasted_iota(jnp.int32, (LANES, LANES), 1)
    upper = jnp.where(li < lj, 1.0, 0.0).astype(_BF16)
    ones = jnp.ones((LANES, LANES), _BF16)
    within = _dotf(x_bf, upper)
    rowtot = _dotf(x_bf, ones)
    ri = lax.broadcasted_iota(jnp.int32, (R, R), 0)
    rj = lax.broadcasted_iota(jnp.int32, (R, R), 1)
    lower = jnp.where(rj < ri, 1.0, 0.0).astype(_BF16)
    return within + _dotf(lower, rowtot.astype(_BF16))


def _select_kernel(aff_ref, cs_ref, cnt_ref, *, cap):
    aff = aff_ref[...]
    R = aff.shape[0]
    bits = pltpu.bitcast(aff, jnp.int32)
    capf = jnp.float32(cap)

    def count_ge(t):
        return jnp.sum(jnp.where(bits >= t, 1.0, 0.0))

    def search(_, lohi):
        lo, hi = lohi
        mid = lo + lax.shift_right_logical(hi - lo, 1)
        ok = count_ge(mid) >= capf
        return jnp.where(ok, mid, lo), jnp.where(ok, hi, mid)

    lo, _ = lax.fori_loop(0, 31, search, (jnp.int32(0), jnp.int32(0x7F800000)))
    gt = bits > lo
    eq = bits == lo
    need = capf - jnp.sum(jnp.where(gt, 1.0, 0.0))
    eq_rank = _excl_cumsum_rows(jnp.where(eq, 1.0, 0.0).astype(_BF16), R)
    sel = jnp.logical_or(gt, jnp.logical_and(eq, eq_rank < need))
    cs = _excl_cumsum_rows(jnp.where(sel, 1.0, 0.0).astype(_BF16), R)
    cs_ref[...] = jnp.where(sel, cs, -1.0)
    cnt_ref[...] = cs.astype(jnp.int32)


def _moe_select(aff_t, cap):
    E, N = aff_t.shape
    R = N // LANES
    kern = functools.partial(_select_kernel, cap=cap)
    spec = pl.BlockSpec((None, R, LANES), lambda e: (e, 0, 0))
    slot, before = pl.pallas_call(
        kern,
        out_shape=(jax.ShapeDtypeStruct((E, R, LANES), _F32), jax.ShapeDtypeStruct((E, R, LANES), jnp.int32)),
        grid=(E,),
        in_specs=[spec],
        out_specs=(spec, spec),
        compiler_params=_cparams(("arbitrary",), VMEM_LIMIT),
        name="moe_select",
    )(aff_t.reshape(E, R, LANES))
    return slot.reshape(E, N), before.reshape(E, N)


COMPACT_TILE = 512


def _compact_kernel(slot_ref, aff_ref, idx_ref, gate_ref):
    k = pl.program_id(1)
    slot = slot_ref[...]
    R = slot.shape[0]
    T = COMPACT_TILE
    m_bf = jnp.where(slot >= 0.0, 1.0, 0.0).astype(_BF16)
    ones8 = jnp.ones((8, LANES), _BF16)
    rowtot = _dot_nt(ones8, m_bf)
    ri = lax.broadcasted_iota(jnp.int32, (R, R), 0)
    rj = lax.broadcasted_iota(jnp.int32, (R, R), 1)
    incl = _dotf(rowtot.astype(_BF16), jnp.where(ri <= rj, 1.0, 0.0).astype(_BF16))
    excl = incl - rowtot
    c = (k * T + lax.broadcasted_iota(jnp.int32, (T, R), 0)).astype(_F32)
    oh = jnp.logical_and(excl[0:1, :] <= c, c < incl[0:1, :])
    ohb = jnp.where(oh, 1.0, 0.0).astype(_BF16)
    li = lax.broadcasted_iota(jnp.int32, (LANES, LANES), 0)
    lj = lax.broadcasted_iota(jnp.int32, (LANES, LANES), 1)
    loc_incl = _dotf(m_bf, jnp.where(li <= lj, 1.0, 0.0).astype(_BF16))
    a_hi, a_mid, a_lo = _split3(aff_ref[...])
    g_cs = _dotf(ohb, loc_incl.astype(_BF16))
    g_aff = _dotf(ohb, a_hi) + (_dotf(ohb, a_mid) + _dotf(ohb, a_lo))
    excl_sel = jnp.sum(jnp.where(oh, excl[0:1, :], 0.0), axis=1, keepdims=True)
    row_sel = jnp.sum(jnp.where(oh, rj[0:1, :].astype(_F32), 0.0), axis=1, keepdims=True)
    target = c[:, 0:1] - excl_sel + 1.0
    lane_sel = jnp.sum(jnp.where(g_cs < target, 1.0, 0.0), axis=1, keepdims=True)
    lane_f = lax.broadcasted_iota(jnp.int32, (T, LANES), 1).astype(_F32)
    gate = jnp.sum(jnp.where(lane_f == lane_sel, g_aff, 0.0), axis=1, keepdims=True)
    idx_ref[...] = jnp.broadcast_to(row_sel * LANES + lane_sel, (T, LANES)).astype(jnp.int32)
    gate_ref[...] = jnp.broadcast_to(gate, (T, LANES))


def _moe_compact(slot, aff_t, cap):
    E, N = slot.shape
    R = N // LANES
    T = COMPACT_TILE
    nt = cap // T
    spec_in = pl.BlockSpec((None, R, LANES), lambda e, k: (e, 0, 0))
    spec_out = pl.BlockSpec((T, LANES), lambda e, k: (e * nt + k, 0))
    return pl.pallas_call(
        _compact_kernel,
        out_shape=(jax.ShapeDtypeStruct((E * cap, LANES), jnp.int32),
                   jax.ShapeDtypeStruct((E * cap, LANES), _F32)),
        grid=(E, nt),
        in_specs=[spec_in, spec_in],
        out_specs=(spec_out, spec_out),
        compiler_params=_cparams(("arbitrary", "arbitrary"), VMEM_LIMIT),
        name="moe_compact",
    )(slot.reshape(E, R, LANES), aff_t.reshape(E, R, LANES))


def _row_copy(h_hbm, xbuf, sem, src_row, dst_row):
    return pltpu.make_async_copy(h_hbm.at[pl.ds(src_row, 1)], xbuf.at[pl.ds(dst_row, 1)], sem)


def _ffn_kernel(idx_ref, idx_next_ref, gate_ref, wg_ref, wu_ref, wd_ref, h_hbm, o_ref, xbuf, xb_ref, sems,
                *, tm, nsteps):
    step = pl.program_id(0)
    slot = lax.rem(step, 2)

    def drain(slot):
        def wait(i, carry):
            _row_copy(h_hbm, xbuf.at[slot], sems.at[slot], 0, i).wait()
            return carry

        lax.fori_loop(0, tm, wait, 0, unroll=8)

    @pl.when(step == 0)
    def _():
        def issue(i, carry):
            _row_copy(h_hbm, xbuf.at[0], sems.at[0], idx_ref[i], i).start()
            return carry

        lax.fori_loop(0, tm, issue, 0, unroll=8)

    drain(slot)
    xb_ref[...] = xbuf[slot].astype(_BF16)
    for i in range(tm):
        _row_copy(h_hbm, xbuf.at[1 - slot], sems.at[1 - slot], idx_next_ref[i], i).start()
    x = xb_ref[...]
    g = _dotf(x, wg_ref[...])
    u = _dotf(x, wu_ref[...])
    mid = (_silu(g) * u).astype(_BF16)
    y = _dotf(mid, wd_ref[...])
    gate = gate_ref[...]
    for j in range(y.shape[1] // LANES):
        o_ref[:, j * LANES:(j + 1) * LANES] = (y[:, j * LANES:(j + 1) * LANES] * gate).astype(o_ref.dtype)

    @pl.when(step == nsteps - 1)
    def _():
        drain(1 - slot)


def _moe_ffn(idx, gate, h, wg, wu, wd, cap, tm):
    E, D, F = wg.shape
    tm = min(tm, cap)
    nt = cap // tm
    nsteps = E * nt
    kern = functools.partial(_ffn_kernel, tm=tm, nsteps=nsteps)
    return pl.pallas_call(
        kern,
        out_shape=jax.ShapeDtypeStruct((E * cap, D), _BF16),
        grid=(nsteps,),
        in_specs=[
            pl.BlockSpec((tm,), lambda s: (s,), memory_space=pltpu.SMEM),
            pl.BlockSpec((tm,), lambda s: (jnp.minimum(s + 1, nsteps - 1),), memory_space=pltpu.SMEM),
            pl.BlockSpec((tm, LANES), lambda s: (s, 0)),
            pl.BlockSpec((None, D, F), lambda s: (s // nt, 0, 0), pipeline_mode=pl.Buffered(1)),
            pl.BlockSpec((None, D, F), lambda s: (s // nt, 0, 0), pipeline_mode=pl.Buffered(1)),
            pl.BlockSpec((None, F, D), lambda s: (s // nt, 0, 0), pipeline_mode=pl.Buffered(1)),
            pl.BlockSpec(memory_space=pl.ANY),
        ],
        out_specs=pl.BlockSpec((tm, D), lambda s: (s, 0)),
        scratch_shapes=[pltpu.VMEM((2, tm, D), _F32), pltpu.VMEM((tm, D), _BF16),
                        pltpu.SemaphoreType.DMA((2,))],
        compiler_params=_cparams(("arbitrary",), VMEM_LIMIT),
        name="moe_ffn",
    )(idx, idx, gate, wg, wu, wd, h)


COMBINE_TOK = 256
COMBINE_WIN = 128


def _combine_kernel(bstart_ref, slot_ref, x_ref, gate_ref, g_ref, b_ref, ye_hbm, o_ref,
                    ybuf, xtra, sems, xsem, acc_ref, *, cap, n_exp, nblk):
    blk = pl.program_id(0)
    par = lax.rem(blk, 2)
    W = COMBINE_WIN

    def aligned_start(b, e):
        s0 = bstart_ref[b * n_exp + e]
        return lax.shift_left(lax.shift_right_logical(s0, 4), 4)

    def window(b, e, j):
        row = e * cap + jnp.minimum(aligned_start(b, e) + j * W, cap - W)
        return pl.multiple_of(row, BF16_SUBLANES)

    def first_windows(b, par):
        return [pltpu.make_async_copy(ye_hbm.at[pl.ds(window(b, e, 0), W)],
                                      ybuf.at[par, pl.ds(e * W, W)], sems.at[par]) for e in range(n_exp)]

    @pl.when(blk == 0)
    def _():
        for cp in first_windows(0, 0):
            cp.start()

    @pl.when(blk + 1 < nblk)
    def _():
        for cp in first_windows(blk + 1, 1 - par):
            cp.start()

    for cp in first_windows(blk, par):
        cp.wait()

    slots = slot_ref[...]
    lane = lax.broadcasted_iota(jnp.int32, (COMBINE_TOK, W), 1)

    def onehot(e, row0, first_slot):
        slot_e = slots[:, e:e + 1].astype(jnp.int32)
        local = jnp.where(slot_e >= first_slot, slot_e - (row0 - e * cap), -1)
        return jnp.where(local == lane, 1.0, 0.0).astype(_BF16)

    placement = jnp.concatenate([onehot(e, window(blk, e, 0), 0) for e in range(n_exp)], axis=1)
    acc_ref[...] = _dotf(placement, ybuf[par])

    for e in range(n_exp):
        al = aligned_start(blk, e)
        s1 = bstart_ref[(blk + 1) * n_exp + e]
        nwin = lax.div(s1 - al + (W - 1), W)

        def extra(j, carry, e=e, al=al):
            r0 = window(blk, e, j)
            cp = pltpu.make_async_copy(ye_hbm.at[pl.ds(r0, W)], xtra, xsem)
            cp.start()
            cp.wait()
            acc_ref[...] += _dotf(onehot(e, r0, al + j * W), xtra[...])
            return carry

        lax.fori_loop(1, nwin, extra, 0)

    y = DEEPNORM_ALPHA * x_ref[...] + gate_ref[...] * acc_ref[...]
    mu = jnp.mean(y, axis=-1, keepdims=True)
    yc = y - mu
    var = jnp.mean(yc * yc, axis=-1, keepdims=True)
    o_ref[...] = yc * lax.rsqrt(var + LN_EPS) * g_ref[...] + b_ref[...]


def _moe_combine(bstart, slot_t, x, gate, ln_g, ln_b, ye, cap):
    B, S, D = x.shape
    N = B * S
    E = slot_t.shape[1]
    T = COMBINE_TOK
    per_seq = S // T
    kern = functools.partial(_combine_kernel, cap=cap, n_exp=E, nblk=N // T)
    out = pl.pallas_call(
        kern,
        out_shape=jax.ShapeDtypeStruct((N, D), _F32),
        grid_spec=pltpu.PrefetchScalarGridSpec(
            num_scalar_prefetch=1,
            grid=(N // T,),
            in_specs=[
                pl.BlockSpec((T, E), lambda i, bs: (i, 0)),
                pl.BlockSpec((T, D), lambda i, bs: (i, 0)),
                pl.BlockSpec((None, 1, D), lambda i, bs: (i // per_seq, 0, 0)),
                pl.BlockSpec((1, D), lambda i, bs: (0, 0)),
                pl.BlockSpec((1, D), lambda i, bs: (0, 0)),
                pl.BlockSpec(memory_space=pl.ANY),
            ],
            out_specs=pl.BlockSpec((T, D), lambda i, bs: (i, 0)),
            scratch_shapes=[
                pltpu.VMEM((2, E * COMBINE_WIN, D), _BF16),
                pltpu.VMEM((COMBINE_WIN, D), _BF16),
                pltpu.SemaphoreType.DMA((2,)),
                pltpu.SemaphoreType.DMA,
                pltpu.VMEM((T, D), _F32),
            ],
        ),
        compiler_params=_cparams(("arbitrary",), VMEM_LIMIT),
        name="moe_combine",
    )(bstart, slot_t, x.reshape(N, D), gate, ln_g, ln_b, ye)
    return out.reshape(B, S, D)


def _expert_choice(x1, h2, aff_t, gate, ln_g, ln_b, wg, wu, wd, ffn_tm):
    B, S, D = x1.shape
    N = B * S
    E = aff_t.shape[0]
    cap = EC_CAPACITY * N // E
    slot, before = _moe_select(aff_t, cap)
    idx_rep, gate_rep = _moe_compact(slot, aff_t, cap)
    ye = _moe_ffn(idx_rep[:, 0], gate_rep, h2.reshape(N, D), wg, wu, wd, cap, ffn_tm)
    bstart = jnp.concatenate([before[:, ::COMBINE_TOK], jnp.full((E, 1), cap, jnp.int32)], axis=1)
    return _moe_combine(bstart.T.reshape(-1), slot.T, x1, gate, ln_g, ln_b, ye, cap)


def _rope_np(n, dim):
    inv = 1.0 / (ROPE_THETA ** (jnp.arange(0, dim, 2, dtype=_F32) / dim))
    ang = jnp.arange(n, dtype=_F32)[:, None] * inv[None, :]
    return jnp.cos(ang), jnp.sin(ang)


def _head_rope_tables(S):
    cos, sin = _rope_np(S, HEAD_DIM)
    c = jnp.tile(cos, (1, LANES // (HEAD_DIM // 2)))
    s = jnp.tile(jnp.concatenate([-sin, sin], axis=1), (1, LANES // HEAD_DIM))
    return c, s


def _mla_rope_tables(S):
    cos, sin = _rope_np(S, MLA_ROPE)
    pad = LANES - MLA_NOPE - MLA_ROPE
    c = jnp.concatenate([jnp.ones((S, MLA_NOPE), _F32), cos, cos, jnp.zeros((S, pad), _F32)], axis=1)
    s = jnp.concatenate([jnp.zeros((S, MLA_NOPE), _F32), -sin, sin, jnp.zeros((S, pad), _F32)], axis=1)
    return c, s


def _prep_ab_weights(w_in, w_q_up, w_kv_up, w_out):
    D = w_in.shape[0]
    s1 = 3 * NA_WIDTH
    s3 = s1 + MLA_Q_RANK + MLA_KV_RANK
    pad = LANES - MLA_NOPE - MLA_ROPE
    krope = jnp.concatenate([jnp.zeros((D, MLA_NOPE), _F32), w_in[:, s3:], jnp.zeros((D, pad), _F32)], axis=1)
    w_in_p = jnp.concatenate([w_in[:, :s3], krope], axis=1).astype(_BF16)
    wq = w_q_up.reshape(MLA_Q_RANK, MLA_HEADS, MLA_NOPE + MLA_ROPE)
    wq = jnp.pad(wq, ((0, 0), (0, 0), (0, pad))).reshape(MLA_Q_RANK, MLA_HEADS * LANES).astype(_BF16)
    wkv = w_kv_up.reshape(MLA_KV_RANK, MLA_HEADS, MLA_NOPE + MLA_V)
    wk = jnp.pad(wkv[:, :, :MLA_NOPE], ((0, 0), (0, 0), (0, LANES - MLA_NOPE)))
    wv = jnp.pad(wkv[:, :, MLA_NOPE:], ((0, 0), (0, 0), (0, LANES - MLA_V)))
    wk = wk.reshape(MLA_KV_RANK, MLA_HEADS * LANES).astype(_BF16)
    wv = wv.reshape(MLA_KV_RANK, MLA_HEADS * LANES).astype(_BF16)
    w_out_na = w_out[:NA_WIDTH].astype(_BF16)
    wo = w_out[NA_WIDTH:].reshape(MLA_HEADS, MLA_V, D)
    w_out_mla = jnp.pad(wo, ((0, 0), (0, LANES - MLA_V), (0, 0))).reshape(MLA_HEADS * LANES, D).astype(_BF16)
    return w_in_p, wq, wk, wv, w_out_na, w_out_mla


def _trunk(x, c_mods, P, layer_w):
    B, S, D = x.shape
    cos_h, sin_h = _head_rope_tables(S)
    cos_r, sin_r = _mla_rope_tables(S)
    for layer in range(DEPTH):
        j = layer // 2
        mm = c_mods[layer]
        lw = layer_w[layer]
        if layer % 2 == 0:
            proj = _inproj(x, mm["mix_shift"], mm["mix_scale"], lw["w_in"], cos_h, sin_h,
                           dil=1, rope_cols=0, half=1, tm=512).reshape(B, S, -1)
            na = _na_attention(proj, lw["bias_tab"])
            wlat = MLA_Q_RANK + MLA_KV_RANK + LANES
            q, k, v = _mla_prep(proj, (3 * NA_WIDTH) // wlat, lw["q_norm"], lw["kv_norm"],
                                lw["wq"], lw["wk"], lw["wv"], cos_r, sin_r, tm=512)
            mla = _mla_flash(q, k, v, tq=256, tk=512)
            acts, weights, kind = [na, mla], [lw["w_out_na"], lw["w_out_mla"]], "ab"
        else:
            acts = []
            for g, (window, dil) in enumerate(DIL_PAIRS):
                assert (window // 2) // dil == DIL_HALF
                qkv = _inproj(x, mm["mix_shift"], mm["mix_scale"], lw["w_in"][g], cos_h, sin_h,
                              dil=dil, rope_cols=2 * DIL_WIDTH, half=HEAD_DIM // 2, tm=512)
                o, lse = _dil_attention(qkv)
                acts += [o, lse]
            weights, kind = [lw["w_out"]], "dil"
        x1, h2, aff_t = _outproj(kind, acts, weights, x, mm["mix_gate"], lw["mix_ln_g"], lw["mix_ln_b"],
                                 mm["moe_shift"], mm["moe_scale"], lw["wr_t"], tm=256)
        x = _expert_choice(x1, h2, aff_t, mm["moe_gate"], lw["moe_ln_g"], lw["moe_ln_b"],
                           lw["wg"], lw["wu"], lw["wd"], ffn_tm=512)
    return x


def kernel(x_prompt, x_sample, c_prompt, c_sample, ab_w_in, na_rpb, mla_q_norm, mla_w_q_up, mla_kv_norm,
           mla_w_kv_up, ab_w_out, dil_w_in, dil_w_out, mix_ada_w, mix_ada_b, mix_ln_g, mix_ln_b,
           moe_ada_w, moe_ada_b, moe_ln_g, moe_ln_b, moe_w_router, moe_w_gate, moe_w_up, moe_w_down):
    D = x_prompt.shape[-1]
    nb_p, nb_s = c_prompt.shape[0], c_sample.shape[0]
    bp = -(-(nb_p + nb_s) // 8) * 8
    c_all = jnp.zeros((bp, D), _F32).at[:nb_p].set(c_prompt).at[nb_p:nb_p + nb_s].set(c_sample)
    mix_mod = _ada_mod(c_all, mix_ada_w, mix_ada_b)
    moe_mod = _ada_mod(c_all, moe_ada_w, moe_ada_b)

    def mods_for(lo, n):
        out = []
        for layer in range(DEPTH):
            d = {}
            for name, mod in (("mix", mix_mod), ("moe", moe_mod)):
                m = mod[layer, lo:lo + n]
                d[name + "_shift"] = m[:, None, :D]
                d[name + "_scale"] = m[:, None, D:2 * D]
                d[name + "_gate"] = m[:, None, 2 * D:]
            out.append(d)
        return out

    layer_w = []
    for layer in range(DEPTH):
        j = layer // 2
        lw = {
            "mix_ln_g": mix_ln_g[layer][None], "mix_ln_b": mix_ln_b[layer][None],
            "moe_ln_g": moe_ln_g[layer][None], "moe_ln_b": moe_ln_b[layer][None],
            "wr_t": moe_w_router[layer].T,
            "wg": moe_w_gate[layer].astype(_BF16), "wu": moe_w_up[layer].astype(_BF16),
            "wd": moe_w_down[layer].astype(_BF16),
        }
        if layer % 2 == 0:
            w_in_p, wq, wk, wv, w_out_na, w_out_mla = _prep_ab_weights(
                ab_w_in[j], mla_w_q_up[j], mla_w_kv_up[j], ab_w_out[j])
            lw.update(w_in=w_in_p, wq=wq, wk=wk, wv=wv, w_out_na=w_out_na, w_out_mla=w_out_mla,
                      q_norm=mla_q_norm[j][None], kv_norm=mla_kv_norm[j][None],
                      bias_tab=_na_bias_table(na_rpb[j]))
        else:
            wi = dil_w_in[j].reshape(D, len(DIL_PAIRS), 3 * DIL_WIDTH).astype(_BF16)
            lw.update(w_in=[wi[:, g] for g in range(len(DIL_PAIRS))], w_out=dil_w_out[j].astype(_BF16))
        layer_w.append(lw)

    y_prompt = _trunk(x_prompt, mods_for(0, nb_p), None, layer_w)
    y_sample = _trunk(x_sample, mods_for(nb_p, nb_s), None, layer_w)
    return (y_prompt, y_sample)
```

```python
import functools

import numpy as np
import jax
import jax.numpy as jnp
from jax import lax
from jax.experimental import pallas as pl
from jax.experimental.pallas import tpu as pltpu

GRID_W = 64
HEAD_DIM = 64
NA_HEADS = 8
NA_WIN_H = 8
NA_WIN_W = 16
MLA_HEADS = 8
MLA_Q_RANK = 256
MLA_KV_RANK = 128
MLA_NOPE = 64
MLA_ROPE = 32
MLA_V = 64
DIL_PAIRS = ((128, 1), (512, 4), (2048, 16))
DIL_HEADS = 8
N_EXPERTS = 16
EC_CAPACITY = 2
ROPE_THETA = 10000.0
LN_EPS = 1e-5
RMS_EPS = 1e-6
DEPTH = 2
DEEPNORM_ALPHA = (2 * DEPTH) ** 0.25
NA_WIDTH = NA_HEADS * HEAD_DIM
DIL_WIDTH = DIL_HEADS * HEAD_DIM

LANES = 128
BF16_SUBLANES = 16
MASK_VALUE = -1e30
VMEM_LIMIT = 56 * 1024 * 1024

_F32 = jnp.float32
_BF16 = jnp.bfloat16


def _cparams(sem, vmem=None):
    return pltpu.CompilerParams(dimension_semantics=sem, vmem_limit_bytes=vmem)


def _dotf(a, b):
    return jnp.dot(a, b, preferred_element_type=_F32)


def _dot_nt(a, b):
    return lax.dot_general(a, b, (((1,), (1,)), ((), ())), preferred_element_type=_F32)


def _split2(x):
    hi = x.astype(_BF16)
    lo = (x - hi.astype(_F32)).astype(_BF16)
    return hi, lo


def _split3(x):
    hi = x.astype(_BF16)
    r = x - hi.astype(_F32)
    mid = r.astype(_BF16)
    lo = (r - mid.astype(_F32)).astype(_BF16)
    return hi, mid, lo


def _silu(x):
    return x / (1.0 + jnp.exp(-x))


def _rope_rotate(x, half, first_half):
    n = x.shape[-1]
    fwd = pltpu.roll(x, n - half, 1)
    bwd = pltpu.roll(x, half, 1)
    return jnp.where(first_half, fwd, bwd)


def _ada_kernel(c_ref, w_ref, b_ref, o_ref):
    a = _silu(c_ref[...])
    ah, al = _split2(a)
    wh, wl = _split2(w_ref[...])
    o_ref[...] = _dotf(ah, wh) + (_dotf(ah, wl) + _dotf(al, wh)) + b_ref[...]


def _ada_mod(c, w, b):
    L, D, N3 = w.shape
    Bp = c.shape[0]
    tn = min(512, N3)
    return pl.pallas_call(
        _ada_kernel,
        out_shape=jax.ShapeDtypeStruct((L, Bp, N3), _F32),
        grid=(L, N3 // tn),
        in_specs=[
            pl.BlockSpec((Bp, D), lambda l, j: (0, 0)),
            pl.BlockSpec((None, D, tn), lambda l, j: (l, 0, j)),
            pl.BlockSpec((None, 1, tn), lambda l, j: (l, 0, j)),
        ],
        out_specs=pl.BlockSpec((None, Bp, tn), lambda l, j: (l, 0, j)),
        compiler_params=_cparams(("arbitrary", "arbitrary")),
        name="ada_mod",
    )(c, w, b.reshape(L, 1, N3))


INPROJ_CHUNK = 2048
INPROJ_ROWS = 512


def _inproj_kernel(x_ref, shift_ref, scale_ref, w_ref, cos_ref, sin_ref, o_ref, *scratch,
                   dil, tm, rs, rope_cols, half):
    if dil == 1:
        x = x_ref[...]
        cos, sin = cos_ref[...], sin_ref[...]
    else:
        (xs_ref,) = scratch
        nlt = xs_ref.shape[0]

        @pl.when(pl.program_id(2) == 0)
        def _():
            for c in range(nlt):
                xs_ref[c] = x_ref[:, c * LANES:(c + 1) * LANES]

        strides = [pl.ds(pl.program_id(2) * rs + q, tm, stride=dil) for q in range(rs)]
        x = jnp.concatenate(
            [jnp.concatenate([xs_ref[c, rows, :] for c in range(nlt)], axis=1) for rows in strides], axis=0)
        cos = jnp.concatenate([cos_ref[rows, :] for rows in strides], axis=0)
        sin = jnp.concatenate([sin_ref[rows, :] for rows in strides], axis=0)
    h = x * (1.0 + scale_ref[...]) + shift_ref[...]
    acc = _dotf(h.astype(_BF16), w_ref[...])
    n = acc.shape[1]

    def store(cols, blk):
        blk = blk.astype(o_ref.dtype)
        for q in range(rs):
            o_ref[q, :, cols] = blk[q * tm:(q + 1) * tm]

    if rope_cols:
        lane = lax.broadcasted_iota(jnp.int32, cos.shape, 1)
        first = (lane % (2 * half)) < half
        for j in range(rope_cols // LANES):
            blk = acc[:, j * LANES:(j + 1) * LANES]
            store(slice(j * LANES, (j + 1) * LANES), blk * cos + _rope_rotate(blk, half, first) * sin)
    store(slice(rope_cols, n), acc[:, rope_cols:])


def _inproj(x, shift, scale, w, cos, sin, *, dil, rope_cols, half, tm):
    B, S, D = x.shape
    N = w.shape[1]
    L = S // dil
    tm = min(tm, L, INPROJ_CHUNK // dil)
    chunk = tm * dil
    rs = max(1, min(dil, INPROJ_ROWS // tm))
    kern = functools.partial(_inproj_kernel, dil=dil, tm=tm, rs=rs, rope_cols=rope_cols, half=half)
    return pl.pallas_call(
        kern,
        out_shape=jax.ShapeDtypeStruct((B, dil, L, N), _BF16),
        grid=(B, L // tm, dil // rs),
        in_specs=[
            pl.BlockSpec((None, chunk, D), lambda b, j, r: (b, j, 0)),
            pl.BlockSpec((None, 1, D), lambda b, j, r: (b, 0, 0)),
            pl.BlockSpec((None, 1, D), lambda b, j, r: (b, 0, 0)),
            pl.BlockSpec((D, N), lambda b, j, r: (0, 0)),
            pl.BlockSpec((chunk, LANES), lambda b, j, r: (j, 0)),
            pl.BlockSpec((chunk, LANES), lambda b, j, r: (j, 0)),
        ],
        out_specs=pl.BlockSpec((None, rs, tm, N), lambda b, j, r: (b, r, j, 0)),
        scratch_shapes=[pltpu.VMEM((D // LANES, chunk, LANES), _F32)] if dil > 1 else [],
        compiler_params=_cparams(("arbitrary",) * 3, VMEM_LIMIT),
        name="inproj",
    )(x, shift, scale, w, cos, sin)


def _na_kernel(q_ref, k_ref, v_ref, bias_ref, o_ref, *, rows):
    nk = NA_WIN_H * GRID_W
    lane = lax.broadcasted_iota(jnp.int32, (GRID_W, LANES), 1)
    low = lane < HEAD_DIM

    def row_step(r, carry):
        r0 = jnp.clip(r - NA_WIN_H // 2, 0, rows - NA_WIN_H)
        case = r - r0
        q = q_ref[pl.ds(pl.multiple_of(r * GRID_W, GRID_W), GRID_W), :]
        kw = k_ref[pl.ds(pl.multiple_of(r0 * GRID_W, GRID_W), nk), :]
        vw = v_ref[pl.ds(pl.multiple_of(r0 * GRID_W, GRID_W), nk), :]
        zero = jnp.zeros_like(q)
        q2 = jnp.concatenate([jnp.where(low, q, zero), jnp.where(low, zero, q)], axis=0)
        q2 = q2 * jnp.asarray(HEAD_DIM ** -0.5, q.dtype)
        s = _dot_nt(q2, kw) + bias_ref[:, case].reshape(2 * GRID_W, nk)
        m = jnp.max(s, axis=1, keepdims=True)
        p = jnp.exp(s - m)
        l = jnp.sum(p, axis=1, keepdims=True)
        o2 = _dotf(p.astype(_BF16), vw) / l
        o = jnp.where(low, o2[:GRID_W], o2[GRID_W:])
        o_ref[pl.ds(pl.multiple_of(r * GRID_W, GRID_W), GRID_W), :] = o.astype(o_ref.dtype)
        return carry

    lax.fori_loop(0, rows, row_step, 0, unroll=4)


def _na_bias_table(rpb):
    kh, kw = NA_WIN_H, NA_WIN_W
    H = rpb.shape[0]
    W = GRID_W
    P = 2 * W
    lead = (W - 1) - (kw - 1)
    row = jnp.pad(rpb.astype(_F32), ((0, 0), (0, 0), (lead, P - lead - (2 * kw - 1))))
    flat = jnp.tile(row, (1, 1, W))[:, :, :W * (P - 1)]
    toep = flat.reshape(H, 2 * kh - 1, W, P - 1)[:, :, :, W - 1:]
    cols = np.arange(W)
    c0 = np.clip(cols - kw // 2, 0, W - kw)
    valid = (cols[None, :] >= c0[:, None]) & (cols[None, :] < c0[:, None] + kw)
    toep = jnp.where(valid[None, None], toep, MASK_VALUE)
    per_case = [jnp.transpose(toep[:, kh - 1 - case:2 * kh - 1 - case], (0, 2, 1, 3)) for case in range(kh)]
    return jnp.stack(per_case, axis=1).reshape(H, kh, W, kh * W)


def _na_attention(proj, bias_tab):
    B, S, _ = proj.shape
    rows = S // GRID_W
    assert rows >= NA_WIN_H
    npair = NA_HEADS // 2
    kern = functools.partial(_na_kernel, rows=rows)
    return pl.pallas_call(
        kern,
        out_shape=jax.ShapeDtypeStruct((B, S, NA_WIDTH), _BF16),
        grid=(B, npair),
        in_specs=[
            pl.BlockSpec((None, S, LANES), lambda b, p: (b, 0, p)),
            pl.BlockSpec((None, S, LANES), lambda b, p: (b, 0, npair + p)),
            pl.BlockSpec((None, S, LANES), lambda b, p: (b, 0, 2 * npair + p)),
            pl.BlockSpec((2, NA_WIN_H, GRID_W, NA_WIN_H * GRID_W), lambda b, p: (p, 0, 0, 0)),
        ],
        out_specs=pl.BlockSpec((None, S, LANES), lambda b, p: (b, 0, p)),
        compiler_params=_cparams(("arbitrary", "arbitrary"), VMEM_LIMIT),
        name="na_attention",
    )(proj, proj, proj, bias_tab)


def _rms(x, g):
    xf = x.astype(_F32)
    return xf * lax.rsqrt(jnp.mean(xf * xf, axis=-1, keepdims=True) + RMS_EPS) * g


def _mla_prep_kernel(lat_ref, qn_ref, kn_ref, wq_ref, wk_ref, wv_ref, cos_ref, sin_ref,
                     q_ref, k_ref, v_ref):
    lat = lat_ref[...]
    qn = _rms(lat[:, :MLA_Q_RANK], qn_ref[...]).astype(_BF16)
    kvn = _rms(lat[:, MLA_Q_RANK:MLA_Q_RANK + MLA_KV_RANK], kn_ref[...]).astype(_BF16)
    kr = lat[:, MLA_Q_RANK + MLA_KV_RANK:].astype(_F32)
    cos = cos_ref[...]
    sin = sin_ref[...]
    lane = lax.broadcasted_iota(jnp.int32, cos.shape, 1)
    half = MLA_ROPE // 2
    first = lane < MLA_NOPE + half
    scale = (MLA_NOPE + MLA_ROPE) ** -0.5
    kpe = kr * cos + _rope_rotate(kr, half, first) * sin
    q_all = _dotf(qn, wq_ref[...])
    k_all = _dotf(kvn, wk_ref[...])
    v_all = _dotf(kvn, wv_ref[...])
    ones_hi = jnp.where(lane >= MLA_V, 1.0, 0.0)
    for h in range(MLA_HEADS):
        sl = slice(h * LANES, (h + 1) * LANES)
        qh = q_all[:, sl]
        qh = (qh * cos + _rope_rotate(qh, half, first) * sin) * scale
        q_ref[:, sl] = qh.astype(q_ref.dtype)
        k_ref[:, sl] = (k_all[:, sl] + kpe).astype(k_ref.dtype)
        v_ref[:, sl] = (v_all[:, sl] + ones_hi).astype(v_ref.dtype)


def _mla_prep(proj, col_block, qn, kn, wq, wk, wv, cos, sin, tm):
    B, S, _ = proj.shape
    wlat = MLA_Q_RANK + MLA_KV_RANK + LANES
    W = MLA_HEADS * LANES
    tm = min(tm, S)
    out = jax.ShapeDtypeStruct((B, S, W), _BF16)
    full = lambda shp: pl.BlockSpec(shp, lambda b, i: (0,) * len(shp))
    return pl.pallas_call(
        _mla_prep_kernel,
        out_shape=(out, out, out),
        grid=(B, S // tm),
        in_specs=[
            pl.BlockSpec((None, tm, wlat), lambda b, i: (b, i, col_block)),
            full((1, MLA_Q_RANK)), full((1, MLA_KV_RANK)),
            full(wq.shape), full(wk.shape), full(wv.shape),
            pl.BlockSpec((tm, LANES), lambda b, i: (i, 0)),
            pl.BlockSpec((tm, LANES), lambda b, i: (i, 0)),
        ],
        out_specs=(pl.BlockSpec((None, tm, W), lambda b, i: (b, i, 0)),) * 3,
        compiler_params=_cparams(("arbitrary", "arbitrary"), VMEM_LIMIT),
        name="mla_prep",
    )(proj, qn, kn, wq, wk, wv, cos, sin)


def _flash_kernel(q_ref, k_ref, v_ref, o_ref, s_ref, m_ref, acc_ref, *, tq, tk):
    S = k_ref.shape[0]
    npair = S // (2 * tk)
    ntile = S // tq

    def reset():
        m_ref[...] = jnp.full(m_ref.shape, MASK_VALUE, _F32)
        acc_ref[...] = jnp.zeros(acc_ref.shape, _F32)

    def scores(tile, i):
        q = q_ref[pl.ds(pl.multiple_of(tile * tq, tq), tq), :]
        return _dot_nt(q, k_ref[pl.ds(pl.multiple_of(i * tk, tk), tk), :])

    def softmax_update(i, slot):
        s = s_ref[slot]
        chunks = [s[:, c * LANES:(c + 1) * LANES] for c in range(tk // LANES)]
        mx = functools.reduce(jnp.maximum, chunks)
        m_old = m_ref[...]
        m_new = jnp.maximum(m_old, jnp.max(mx, axis=1, keepdims=True))
        p = jnp.concatenate([jnp.exp((c - m_new).astype(_BF16)) for c in chunks], axis=1)
        pv = _dotf(p, v_ref[pl.ds(pl.multiple_of(i * tk, tk), tk), :])
        acc_ref[...] = acc_ref[...] * jnp.exp(m_old - m_new) + pv
        m_ref[...] = m_new

    reset()
    s_ref[0] = scores(0, 0)

    def kv_pair(t, carry):
        tile = lax.div(t, npair)
        i = 2 * lax.rem(t, npair)
        last = i == 2 * (npair - 1)
        s_ref[1] = scores(tile, i + 1)
        softmax_update(i, 0)
        s_ref[0] = scores(jnp.where(last, jnp.minimum(tile + 1, ntile - 1), tile), jnp.where(last, 0, i + 2))
        softmax_update(i + 1, 1)

        @pl.when(last)
        def _():
            acc = acc_ref[...]
            lane = lax.broadcasted_iota(jnp.int32, acc.shape, 1)
            den = jnp.where(lane < MLA_V, pltpu.roll(acc, MLA_V, 1), acc)
            o_ref[pl.ds(pl.multiple_of(tile * tq, tq), tq), :] = (acc / den).astype(o_ref.dtype)
            reset()

        return carry

    lax.fori_loop(0, ntile * npair, kv_pair, 0)


def _mla_flash(q, k, v, tq, tk):
    B, S, W = q.shape
    tq = min(tq, S)
    tk = min(tk, S // 2)
    assert S % (2 * tk) == 0 and S % tq == 0
    kern = functools.partial(_flash_kernel, tq=tq, tk=tk)
    spec = lambda: pl.BlockSpec((None, S, LANES), lambda b, h: (b, 0, h))
    return pl.pallas_call(
        kern,
        out_shape=jax.ShapeDtypeStruct((B, S, W), _BF16),
        grid=(B, MLA_HEADS),
        in_specs=[spec(), spec(), spec()],
        out_specs=spec(),
        scratch_shapes=[pltpu.VMEM((2, tq, tk), _F32), pltpu.VMEM((tq, LANES), _F32),
                        pltpu.VMEM((tq, LANES), _F32)],
        compiler_params=_cparams(("arbitrary",) * 2, VMEM_LIMIT),
        name="mla_flash",
    )(q, k, v)


DIL_QBLK = 128
DIL_HALF = 64
DIL_KWIN = DIL_QBLK + 2 * DIL_HALF
DIL_ALL_PAIRS_MAX_LEN = 2048


def _dil_mask_table():
    qi = np.arange(2 * DIL_QBLK)[None, :, None] % DIL_QBLK
    ki = np.arange(DIL_KWIN)[None, None, :]
    back = (np.arange(3) * DIL_HALF)[:, None, None]
    valid = np.abs(ki - back - qi) <= DIL_HALF
    return jnp.asarray(np.where(valid, 0.0, MASK_VALUE), _F32)


def _dil_kernel(q_ref, k_ref, v_ref, mask_ref, o_ref, lse_ref, *, L):
    lane = lax.broadcasted_iota(jnp.int32, (DIL_QBLK, LANES), 1)
    low = lane < HEAD_DIM

    def blk_step(i, carry, cols):
        q0 = pl.multiple_of(i * DIL_QBLK, DIL_QBLK)
        start = pl.multiple_of(jnp.clip(q0 - DIL_HALF, 0, L - DIL_KWIN), DIL_HALF)
        q = q_ref[pl.ds(q0, DIL_QBLK), cols]
        kw = k_ref[pl.ds(start, DIL_KWIN), cols]
        vw = v_ref[pl.ds(start, DIL_KWIN), cols]
        zero = jnp.zeros_like(q)
        q2 = jnp.concatenate([jnp.where(low, q, zero), jnp.where(low, zero, q)], axis=0)
        q2 = q2 * jnp.asarray(HEAD_DIM ** -0.5, q.dtype)
        s = _dot_nt(q2, kw) + mask_ref[lax.div(q0 - start, DIL_HALF)]
        m = jnp.max(s, axis=1, keepdims=True)
        p = jnp.exp(s - m)
        l = jnp.sum(p, axis=1, keepdims=True)
        o2 = _dotf(p.astype(_BF16), vw) / l
        lse2 = m + jnp.log(l)
        o_ref[pl.ds(q0, DIL_QBLK), cols] = jnp.where(low, o2[:DIL_QBLK], o2[DIL_QBLK:]).astype(o_ref.dtype)
        lse_ref[pl.ds(q0, DIL_QBLK), cols] = jnp.where(low, lse2[:DIL_QBLK], lse2[DIL_QBLK:])
        return carry

    nblk = L // DIL_QBLK
    for pair in range(q_ref.shape[1] // LANES):
        step = functools.partial(blk_step, cols=slice(pair * LANES, (pair + 1) * LANES))
        lax.fori_loop(0, nblk, step, 0, unroll=4 if nblk % 4 == 0 else 2)


def _dil_attention(qkv):
    B, d, L, _ = qkv.shape
    assert L >= DIL_KWIN and L % DIL_QBLK == 0
    npair = DIL_HEADS // 2
    pp = npair if L <= DIL_ALL_PAIRS_MAX_LEN else 1
    ng = npair // pp
    w = pp * LANES
    kern = functools.partial(_dil_kernel, L=L)
    return pl.pallas_call(
        kern,
        out_shape=(jax.ShapeDtypeStruct((B, d, L, DIL_WIDTH), _BF16),
                   jax.ShapeDtypeStruct((B, d, L, DIL_WIDTH), _F32)),
        grid=(B, d, ng),
        in_specs=[
            pl.BlockSpec((None, None, L, w), lambda b, r, p: (b, r, 0, p)),
            pl.BlockSpec((None, None, L, w), lambda b, r, p: (b, r, 0, ng + p)),
            pl.BlockSpec((None, None, L, w), lambda b, r, p: (b, r, 0, 2 * ng + p)),
            pl.BlockSpec((3, 2 * DIL_QBLK, DIL_KWIN), lambda b, r, p: (0, 0, 0)),
        ],
        out_specs=(pl.BlockSpec((None, None, L, w), lambda b, r, p: (b, r, 0, p)),) * 2,
        compiler_params=_cparams(("arbitrary",) * 3, VMEM_LIMIT),
        name="dil_attention",
    )(qkv, qkv, qkv, _dil_mask_table())


def _post_mixer(out, x_ref, gate_ref, g_ref, b_ref, shift2_ref, scale2_ref, wr_ref,
                xo_ref, h_ref, aff_ref):
    y = DEEPNORM_ALPHA * x_ref[...] + gate_ref[...] * out
    mu = jnp.mean(y, axis=-1, keepdims=True)
    yc = y - mu
    var = jnp.mean(yc * yc, axis=-1, keepdims=True)
    xn = yc * lax.rsqrt(var + LN_EPS) * g_ref[...] + b_ref[...]
    xo_ref[...] = xn
    h = xn * (1.0 + scale2_ref[...]) + shift2_ref[...]
    h_ref[...] = h
    hh, hl = _split2(h)
    wh, wl = _split2(wr_ref[...])
    logits = _dot_nt(wh, hh) + (_dot_nt(wh, hl) + _dot_nt(wl, hh))
    mx = jnp.max(logits, axis=0, keepdims=True)
    ex = jnp.exp(logits - mx)
    aff_ref[...] = ex / jnp.sum(ex, axis=0, keepdims=True)


def _outproj_ab_kernel(a1_ref, a2_ref, w1_ref, w2_ref, *rest):
    out = _dotf(a1_ref[...], w1_ref[...]) + _dotf(a2_ref[...], w2_ref[...])
    _post_mixer(out, *rest)


def _outproj_dil_kernel(o1, l1, o2, l2, o3, l3, w_ref, *rest):
    *rest, obuf, lbuf = rest
    nlt = obuf.shape[1]
    for g, (o_ref, l_ref) in enumerate(((o1, l1), (o2, l2), (o3, l3))):
        d, n, _ = o_ref.shape
        for r in range(d):
            rows = slice(None) if d == 1 else pl.ds(r, n, stride=d)
            o_r = o_ref[r].astype(_F32)
            l_r = l_ref[r]
            for c in range(nlt):
                obuf[g, c, rows, :] = o_r[:, c * LANES:(c + 1) * LANES]
                lbuf[g, c, rows, :] = l_r[:, c * LANES:(c + 1) * LANES]
    mixed = []
    for c in range(nlt):
        ls = [lbuf[g, c] for g in range(3)]
        mx = jnp.maximum(jnp.maximum(ls[0], ls[1]), ls[2])
        ws = [jnp.exp(l - mx) for l in ls]
        den = ws[0] + ws[1] + ws[2]
        mixed.append(sum((w / den) * obuf[g, c] for g, w in enumerate(ws)))
    out = _dotf(jnp.concatenate(mixed, axis=1).astype(_BF16), w_ref[...])
    _post_mixer(out, *rest)


def _outproj(kind, acts, weights, x, gate, ln_g, ln_b, shift2, scale2, wr_t, tm):
    B, S, D = x.shape
    E = wr_t.shape[0]
    tm = min(tm, S)
    nt = S // tm
    tok = lambda w: pl.BlockSpec((None, tm, w), lambda b, i: (b, i, 0))
    per_b = pl.BlockSpec((None, 1, D), lambda b, i: (b, 0, 0))
    full = lambda shp: pl.BlockSpec(shp, lambda b, i: (0,) * len(shp))
    if kind == "ab":
        kern, scratch = _outproj_ab_kernel, []
        in_specs = [tok(a.shape[-1]) for a in acts]
    else:
        kern = _outproj_dil_kernel
        in_specs = [pl.BlockSpec((None, a.shape[1], tm // a.shape[1], a.shape[3]), lambda b, i: (b, 0, i, 0))
                    for a in acts]
        scratch = [pltpu.VMEM((len(acts) // 2, acts[0].shape[3] // LANES, tm, LANES), _F32)] * 2
    in_specs += [full(w.shape) for w in weights]
    in_specs += [tok(D), per_b, full((1, D)), full((1, D)), per_b, per_b, full(wr_t.shape)]
    return pl.pallas_call(
        kern,
        out_shape=(jax.ShapeDtypeStruct((B, S, D), _F32),
                   jax.ShapeDtypeStruct((B, S, D), _F32),
                   jax.ShapeDtypeStruct((E, B * S), _F32)),
        grid=(B, nt),
        in_specs=in_specs,
        out_specs=(tok(D), tok(D), pl.BlockSpec((E, tm), lambda b, i: (0, b * nt + i))),
        scratch_shapes=scratch,
        compiler_params=_cparams(("arbitrary", "arbitrary"), VMEM_LIMIT),
        name="outproj_" + kind,
    )(*acts, *weights, x, gate, ln_g, ln_b, shift2, scale2, wr_t)


def _excl_cumsum_rows(x_bf, R):
    li = lax.broadcasted_iota(jnp.int32, (LANES, LANES), 0)
    lj = lax.broadcasted_iota(jnp.int32, (LANES, LANES), 1)
    upper = jnp.where(li < lj, 1.0, 0.0).astype(_BF16)
    ones = jnp.ones((LANES, LANES), _BF16)
    within = _dotf(x_bf, upper)
    rowtot = _dotf(x_bf, ones)
    ri = lax.broadcasted_iota(jnp.int32, (R, R), 0)
    rj = lax.broadcasted_iota(jnp.int32, (R, R), 1)
    lower = jnp.where(rj < ri, 1.0, 0.0).astype(_BF16)
    return within + _dotf(lower, rowtot.astype(_BF16))


def _select_kernel(aff_ref, cs_ref, cnt_ref, *, cap):
    aff = aff_ref[...]
    R = aff.shape[0]
    bits = pltpu.bitcast(aff, jnp.int32)
    capf = jnp.float32(cap)

    def count_ge(t):
        return jnp.sum(jnp.where(bits >= t, 1.0, 0.0))

    def search(_, lohi):
        lo, hi = lohi
        mid = lo + lax.shift_right_logical(hi - lo, 1)
        ok = count_ge(mid) >= capf
        return jnp.where(ok, mid, lo), jnp.where(ok, hi, mid)

    lo, _ = lax.fori_loop(0, 31, search, (jnp.int32(0), jnp.int32(0x7F800000)))
    gt = bits > lo
    eq = bits == lo
    need = capf - jnp.sum(jnp.where(gt, 1.0, 0.0))
    eq_rank = _excl_cumsum_rows(jnp.where(eq, 1.0, 0.0).astype(_BF16), R)
    sel = jnp.logical_or(gt, jnp.logical_and(eq, eq_rank < need))
    cs = _excl_cumsum_rows(jnp.where(sel, 1.0, 0.0).astype(_BF16), R)
    cs_ref[...] = jnp.where(sel, cs, -1.0)
    cnt_ref[...] = cs.astype(jnp.int32)


def _moe_select(aff_t, cap):
    E, N = aff_t.shape
    R = N // LANES
    kern = functools.partial(_select_kernel, cap=cap)
    spec = pl.BlockSpec((None, R, LANES), lambda e: (e, 0, 0))
    slot, before = pl.pallas_call(
        kern,
        out_shape=(jax.ShapeDtypeStruct((E, R, LANES), _F32), jax.ShapeDtypeStruct((E, R, LANES), jnp.int32)),
        grid=(E,),
        in_specs=[spec],
        out_specs=(spec, spec),
        compiler_params=_cparams(("arbitrary",), VMEM_LIMIT),
        name="moe_select",
    )(aff_t.reshape(E, R, LANES))
    return slot.reshape(E, N), before.reshape(E, N)


COMPACT_TILE = 512


def _compact_kernel(slot_ref, aff_ref, idx_ref, gate_ref):
    k = pl.program_id(1)
    slot = slot_ref[...]
    R = slot.shape[0]
    T = COMPACT_TILE
    m_bf = jnp.where(slot >= 0.0, 1.0, 0.0).astype(_BF16)
    ones8 = jnp.ones((8, LANES), _BF16)
    rowtot = _dot_nt(ones8, m_bf)
    ri = lax.broadcasted_iota(jnp.int32, (R, R), 0)
    rj = lax.broadcasted_iota(jnp.int32, (R, R), 1)
    incl = _dotf(rowtot.astype(_BF16), jnp.where(ri <= rj, 1.0, 0.0).astype(_BF16))
    excl = incl - rowtot
    c = (k * T + lax.broadcasted_iota(jnp.int32, (T, R), 0)).astype(_F32)
    oh = jnp.logical_and(excl[0:1, :] <= c, c < incl[0:1, :])
    ohb = jnp.where(oh, 1.0, 0.0).astype(_BF16)
    li = lax.broadcasted_iota(jnp.int32, (LANES, LANES), 0)
    lj = lax.broadcasted_iota(jnp.int32, (LANES, LANES), 1)
    loc_incl = _dotf(m_bf, jnp.where(li <= lj, 1.0, 0.0).astype(_BF16))
    a_hi, a_mid, a_lo = _split3(aff_ref[...])
    g_cs = _dotf(ohb, loc_incl.astype(_BF16))
    g_aff = _dotf(ohb, a_hi) + (_dotf(ohb, a_mid) + _dotf(ohb, a_lo))
    excl_sel = jnp.sum(jnp.where(oh, excl[0:1, :], 0.0), axis=1, keepdims=True)
    row_sel = jnp.sum(jnp.where(oh, rj[0:1, :].astype(_F32), 0.0), axis=1, keepdims=True)
    target = c[:, 0:1] - excl_sel + 1.0
    lane_sel = jnp.sum(jnp.where(g_cs < target, 1.0, 0.0), axis=1, keepdims=True)
    lane_f = lax.broadcasted_iota(jnp.int32, (T, LANES), 1).astype(_F32)
    gate = jnp.sum(jnp.where(lane_f == lane_sel, g_aff, 0.0), axis=1, keepdims=True)
    idx_ref[...] = jnp.broadcast_to(row_sel * LANES + lane_sel, (T, LANES)).astype(jnp.int32)
    gate_ref[...] = jnp.broadcast_to(gate, (T, LANES))


def _moe_compact(slot, aff_t, cap):
    E, N = slot.shape
    R = N // LANES
    T = COMPACT_TILE
    nt = cap // T
    spec_in = pl.BlockSpec((None, R, LANES), lambda e, k: (e, 0, 0))
    spec_out = pl.BlockSpec((T, LANES), lambda e, k: (e * nt + k, 0))
    return pl.pallas_call(
        _compact_kernel,
        out_shape=(jax.ShapeDtypeStruct((E * cap, LANES), jnp.int32),
                   jax.ShapeDtypeStruct((E * cap, LANES), _F32)),
        grid=(E, nt),
        in_specs=[spec_in, spec_in],
        out_specs=(spec_out, spec_out),
        compiler_params=_cparams(("arbitrary", "arbitrary"), VMEM_LIMIT),
        name="moe_compact",
    )(slot.reshape(E, R, LANES), aff_t.reshape(E, R, LANES))


def _row_copy(h_hbm, xbuf, sem, src_row, dst_row):
    return pltpu.make_async_copy(h_hbm.at[pl.ds(src_row, 1)], xbuf.at[pl.ds(dst_row, 1)], sem)


def _ffn_kernel(idx_ref, idx_next_ref, gate_ref, wg_ref, wu_ref, wd_ref, h_hbm, o_ref, xbuf, xb_ref, sems,
                *, tm, nsteps):
    step = pl.program_id(0)
    slot = lax.rem(step, 2)

    def drain(slot):
        def wait(i, carry):
            _row_copy(h_hbm, xbuf.at[slot], sems.at[slot], 0, i).wait()
            return carry

        lax.fori_loop(0, tm, wait, 0, unroll=8)

    @pl.when(step == 0)
    def _():
        def issue(i, carry):
            _row_copy(h_hbm, xbuf.at[0], sems.at[0], idx_ref[i], i).start()
            return carry

        lax.fori_loop(0, tm, issue, 0, unroll=8)

    drain(slot)
    xb_ref[...] = xbuf[slot].astype(_BF16)
    for i in range(tm):
        _row_copy(h_hbm, xbuf.at[1 - slot], sems.at[1 - slot], idx_next_ref[i], i).start()
    x = xb_ref[...]
    g = _dotf(x, wg_ref[...])
    u = _dotf(x, wu_ref[...])
    mid = (_silu(g) * u).astype(_BF16)
    y = _dotf(mid, wd_ref[...])
    gate = gate_ref[...]
    for j in range(y.shape[1] // LANES):
        o_ref[:, j * LANES:(j + 1) * LANES] = (y[:, j * LANES:(j + 1) * LANES] * gate).astype(o_ref.dtype)

    @pl.when(step == nsteps - 1)
    def _():
        drain(1 - slot)


def _moe_ffn(idx, gate, h, wg, wu, wd, cap, tm):
    E, D, F = wg.shape
    tm = min(tm, cap)
    nt = cap // tm
    nsteps = E * nt
    kern = functools.partial(_ffn_kernel, tm=tm, nsteps=nsteps)
    return pl.pallas_call(
        kern,
        out_shape=jax.ShapeDtypeStruct((E * cap, D), _BF16),
        grid=(nsteps,),
        in_specs=[
            pl.BlockSpec((tm,), lambda s: (s,), memory_space=pltpu.SMEM),
            pl.BlockSpec((tm,), lambda s: (jnp.minimum(s + 1, nsteps - 1),), memory_space=pltpu.SMEM),
            pl.BlockSpec((tm, LANES), lambda s: (s, 0)),
            pl.BlockSpec((None, D, F), lambda s: (s // nt, 0, 0), pipeline_mode=pl.Buffered(1)),
            pl.BlockSpec((None, D, F), lambda s: (s // nt, 0, 0), pipeline_mode=pl.Buffered(1)),
            pl.BlockSpec((None, F, D), lambda s: (s // nt, 0, 0), pipeline_mode=pl.Buffered(1)),
            pl.BlockSpec(memory_space=pl.ANY),
        ],
        out_specs=pl.BlockSpec((tm, D), lambda s: (s, 0)),
        scratch_shapes=[pltpu.VMEM((2, tm, D), _F32), pltpu.VMEM((tm, D), _BF16),
                        pltpu.SemaphoreType.DMA((2,))],
        compiler_params=_cparams(("arbitrary",), VMEM_LIMIT),
        name="moe_ffn",
    )(idx, idx, gate, wg, wu, wd, h)


COMBINE_TOK = 256
COMBINE_WIN = 128


def _combine_kernel(bstart_ref, slot_ref, x_ref, gate_ref, g_ref, b_ref, ye_hbm, o_ref,
                    ybuf, xtra, sems, xsem, acc_ref, *, cap, n_exp, nblk):
    blk = pl.program_id(0)
    par = lax.rem(blk, 2)
    W = COMBINE_WIN

    def aligned_start(b, e):
        s0 = bstart_ref[b * n_exp + e]
        return lax.shift_left(lax.shift_right_logical(s0, 4), 4)

    def window(b, e, j):
        row = e * cap + jnp.minimum(aligned_start(b, e) + j * W, cap - W)
        return pl.multiple_of(row, BF16_SUBLANES)

    def first_windows(b, par):
        return [pltpu.make_async_copy(ye_hbm.at[pl.ds(window(b, e, 0), W)],
                                      ybuf.at[par, pl.ds(e * W, W)], sems.at[par]) for e in range(n_exp)]

    @pl.when(blk == 0)
    def _():
        for cp in first_windows(0, 0):
            cp.start()

    @pl.when(blk + 1 < nblk)
    def _():
        for cp in first_windows(blk + 1, 1 - par):
            cp.start()

    for cp in first_windows(blk, par):
        cp.wait()

    slots = slot_ref[...]
    lane = lax.broadcasted_iota(jnp.int32, (COMBINE_TOK, W), 1)

    def onehot(e, row0, first_slot):
        slot_e = slots[:, e:e + 1].astype(jnp.int32)
        local = jnp.where(slot_e >= first_slot, slot_e - (row0 - e * cap), -1)
        return jnp.where(local == lane, 1.0, 0.0).astype(_BF16)

    placement = jnp.concatenate([onehot(e, window(blk, e, 0), 0) for e in range(n_exp)], axis=1)
    acc_ref[...] = _dotf(placement, ybuf[par])

    for e in range(n_exp):
        al = aligned_start(blk, e)
        s1 = bstart_ref[(blk + 1) * n_exp + e]
        nwin = lax.div(s1 - al + (W - 1), W)

        def extra(j, carry, e=e, al=al):
            r0 = window(blk, e, j)
            cp = pltpu.make_async_copy(ye_hbm.at[pl.ds(r0, W)], xtra, xsem)
            cp.start()
            cp.wait()
            acc_ref[...] += _dotf(onehot(e, r0, al + j * W), xtra[...])
            return carry

        lax.fori_loop(1, nwin, extra, 0)

    y = DEEPNORM_ALPHA * x_ref[...] + gate_ref[...] * acc_ref[...]
    mu = jnp.mean(y, axis=-1, keepdims=True)
    yc = y - mu
    var = jnp.mean(yc * yc, axis=-1, keepdims=True)
    o_ref[...] = yc * lax.rsqrt(var + LN_EPS) * g_ref[...] + b_ref[...]


def _moe_combine(bstart, slot_t, x, gate, ln_g, ln_b, ye, cap):
    B, S, D = x.shape
    N = B * S
    E = slot_t.shape[1]
    T = COMBINE_TOK
    per_seq = S // T
    kern = functools.partial(_combine_kernel, cap=cap, n_exp=E, nblk=N // T)
    out = pl.pallas_call(
        kern,
        out_shape=jax.ShapeDtypeStruct((N, D), _F32),
        grid_spec=pltpu.PrefetchScalarGridSpec(
            num_scalar_prefetch=1,
            grid=(N // T,),
            in_specs=[
                pl.BlockSpec((T, E), lambda i, bs: (i, 0)),
                pl.BlockSpec((T, D), lambda i, bs: (i, 0)),
                pl.BlockSpec((None, 1, D), lambda i, bs: (i // per_seq, 0, 0)),
                pl.BlockSpec((1, D), lambda i, bs: (0, 0)),
                pl.BlockSpec((1, D), lambda i, bs: (0, 0)),
                pl.BlockSpec(memory_space=pl.ANY),
            ],
            out_specs=pl.BlockSpec((T, D), lambda i, bs: (i, 0)),
            scratch_shapes=[
                pltpu.VMEM((2, E * COMBINE_WIN, D), _BF16),
                pltpu.VMEM((COMBINE_WIN, D), _BF16),
                pltpu.SemaphoreType.DMA((2,)),
                pltpu.SemaphoreType.DMA,
                pltpu.VMEM((T, D), _F32),
            ],
        ),
        compiler_params=_cparams(("arbitrary",), VMEM_LIMIT),
        name="moe_combine",
    )(bstart, slot_t, x.reshape(N, D), gate, ln_g, ln_b, ye)
    return out.reshape(B, S, D)


def _expert_choice(x1, h2, aff_t, gate, ln_g, ln_b, wg, wu, wd, ffn_tm):
    B, S, D = x1.shape
    N = B * S
    E = aff_t.shape[0]
    cap = EC_CAPACITY * N // E
    slot, before = _moe_select(aff_t, cap)
    idx_rep, gate_rep = _moe_compact(slot, aff_t, cap)
    ye = _moe_ffn(idx_rep[:, 0], gate_rep, h2.reshape(N, D), wg, wu, wd, cap, ffn_tm)
    bstart = jnp.concatenate([before[:, ::COMBINE_TOK], jnp.full((E, 1), cap, jnp.int32)], axis=1)
    return _moe_combine(bstart.T.reshape(-1), slot.T, x1, gate, ln_g, ln_b, ye, cap)


def _rope_np(n, dim):
    inv = 1.0 / (ROPE_THETA ** (jnp.arange(0, dim, 2, dtype=_F32) / dim))
    ang = jnp.arange(n, dtype=_F32)[:, None] * inv[None, :]
    return jnp.cos(ang), jnp.sin(ang)


def _head_rope_tables(S):
    cos, sin = _rope_np(S, HEAD_DIM)
    c = jnp.tile(cos, (1, LANES // (HEAD_DIM // 2)))
    s = jnp.tile(jnp.concatenate([-sin, sin], axis=1), (1, LANES // HEAD_DIM))
    return c, s


def _mla_rope_tables(S):
    cos, sin = _rope_np(S, MLA_ROPE)
    pad = LANES - MLA_NOPE - MLA_ROPE
    c = jnp.concatenate([jnp.ones((S, MLA_NOPE), _F32), cos, cos, jnp.zeros((S, pad), _F32)], axis=1)
    s = jnp.concatenate([jnp.zeros((S, MLA_NOPE), _F32), -sin, sin, jnp.zeros((S, pad), _F32)], axis=1)
    return c, s


def _prep_ab_weights(w_in, w_q_up, w_kv_up, w_out):
    D = w_in.shape[0]
    s1 = 3 * NA_WIDTH
    s3 = s1 + MLA_Q_RANK + MLA_KV_RANK
    pad = LANES - MLA_NOPE - MLA_ROPE
    krope = jnp.concatenate([jnp.zeros((D, MLA_NOPE), _F32), w_in[:, s3:], jnp.zeros((D, pad), _F32)], axis=1)
    w_in_p = jnp.concatenate([w_in[:, :s3], krope], axis=1).astype(_BF16)
    wq = w_q_up.reshape(MLA_Q_RANK, MLA_HEADS, MLA_NOPE + MLA_ROPE)
    wq = jnp.pad(wq, ((0, 0), (0, 0), (0, pad))).reshape(MLA_Q_RANK, MLA_HEADS * LANES).astype(_BF16)
    wkv = w_kv_up.reshape(MLA_KV_RANK, MLA_HEADS, MLA_NOPE + MLA_V)
    wk = jnp.pad(wkv[:, :, :MLA_NOPE], ((0, 0), (0, 0), (0, LANES - MLA_NOPE)))
    wv = jnp.pad(wkv[:, :, MLA_NOPE:], ((0, 0), (0, 0), (0, LANES - MLA_V)))
    wk = wk.reshape(MLA_KV_RANK, MLA_HEADS * LANES).astype(_BF16)
    wv = wv.reshape(MLA_KV_RANK, MLA_HEADS * LANES).astype(_BF16)
    w_out_na = w_out[:NA_WIDTH].astype(_BF16)
    wo = w_out[NA_WIDTH:].reshape(MLA_HEADS, MLA_V, D)
    w_out_mla = jnp.pad(wo, ((0, 0), (0, LANES - MLA_V), (0, 0))).reshape(MLA_HEADS * LANES, D).astype(_BF16)
    return w_in_p, wq, wk, wv, w_out_na, w_out_mla


def _trunk(x, c_mods, P, layer_w):
    B, S, D = x.shape
    cos_h, sin_h = _head_rope_tables(S)
    cos_r, sin_r = _mla_rope_tables(S)
    for layer in range(DEPTH):
        j = layer // 2
        mm = c_mods[layer]
        lw = layer_w[layer]
        if layer % 2 == 0:
            proj = _inproj(x, mm["mix_shift"], mm["mix_scale"], lw["w_in"], cos_h, sin_h,
                           dil=1, rope_cols=0, half=1, tm=512).reshape(B, S, -1)
            na = _na_attention(proj, lw["bias_tab"])
            wlat = MLA_Q_RANK + MLA_KV_RANK + LANES
            q, k, v = _mla_prep(proj, (3 * NA_WIDTH) // wlat, lw["q_norm"], lw["kv_norm"],
                                lw["wq"], lw["wk"], lw["wv"], cos_r, sin_r, tm=512)
            mla = _mla_flash(q, k, v, tq=512, tk=1024)
            acts, weights, kind = [na, mla], [lw["w_out_na"], lw["w_out_mla"]], "ab"
        else:
            acts = []
            for g, (window, dil) in enumerate(DIL_PAIRS):
                assert (window // 2) // dil == DIL_HALF
                qkv = _inproj(x, mm["mix_shift"], mm["mix_scale"], lw["w_in"][g], cos_h, sin_h,
                              dil=dil, rope_cols=2 * DIL_WIDTH, half=HEAD_DIM // 2, tm=512)
                o, lse = _dil_attention(qkv)
                acts += [o, lse]
            weights, kind = [lw["w_out"]], "dil"
        x1, h2, aff_t = _outproj(kind, acts, weights, x, mm["mix_gate"], lw["mix_ln_g"], lw["mix_ln_b"],
                                 mm["moe_shift"], mm["moe_scale"], lw["wr_t"], tm=256)
        x = _expert_choice(x1, h2, aff_t, mm["moe_gate"], lw["moe_ln_g"], lw["moe_ln_b"],
                           lw["wg"], lw["wu"], lw["wd"], ffn_tm=512)
    return x


def kernel(x_prompt, x_sample, c_prompt, c_sample, ab_w_in, na_rpb, mla_q_norm, mla_w_q_up, mla_kv_norm,
           mla_w_kv_up, ab_w_out, dil_w_in, dil_w_out, mix_ada_w, mix_ada_b, mix_ln_g, mix_ln_b,
           moe_ada_w, moe_ada_b, moe_ln_g, moe_ln_b, moe_w_router, moe_w_gate, moe_w_up, moe_w_down):
    D = x_prompt.shape[-1]
    nb_p, nb_s = c_prompt.shape[0], c_sample.shape[0]
    bp = -(-(nb_p + nb_s) // 8) * 8
    c_all = jnp.zeros((bp, D), _F32).at[:nb_p].set(c_prompt).at[nb_p:nb_p + nb_s].set(c_sample)
    mix_mod = _ada_mod(c_all, mix_ada_w, mix_ada_b)
    moe_mod = _ada_mod(c_all, moe_ada_w, moe_ada_b)

    def mods_for(lo, n):
        out = []
        for layer in range(DEPTH):
            d = {}
            for name, mod in (("mix", mix_mod), ("moe", moe_mod)):
                m = mod[layer, lo:lo + n]
                d[name + "_shift"] = m[:, None, :D]
                d[name + "_scale"] = m[:, None, D:2 * D]
                d[name + "_gate"] = m[:, None, 2 * D:]
            out.append(d)
        return out

    layer_w = []
    for layer in range(DEPTH):
        j = layer // 2
        lw = {
            "mix_ln_g": mix_ln_g[layer][None], "mix_ln_b": mix_ln_b[layer][None],
            "moe_ln_g": moe_ln_g[layer][None], "moe_ln_b": moe_ln_b[layer][None],
            "wr_t": moe_w_router[layer].T,
            "wg": moe_w_gate[layer].astype(_BF16), "wu": moe_w_up[layer].astype(_BF16),
            "wd": moe_w_down[layer].astype(_BF16),
        }
        if layer % 2 == 0:
            w_in_p, wq, wk, wv, w_out_na, w_out_mla = _prep_ab_weights(
                ab_w_in[j], mla_w_q_up[j], mla_w_kv_up[j], ab_w_out[j])
            lw.update(w_in=w_in_p, wq=wq, wk=wk, wv=wv, w_out_na=w_out_na, w_out_mla=w_out_mla,
                      q_norm=mla_q_norm[j][None], kv_norm=mla_kv_norm[j][None],
                      bias_tab=_na_bias_table(na_rpb[j]))
        else:
            wi = dil_w_in[j].reshape(D, len(DIL_PAIRS), 3 * DIL_WIDTH).astype(_BF16)
            lw.update(w_in=[wi[:, g] for g in range(len(DIL_PAIRS))], w_out=dil_w_out[j].astype(_BF16))
        layer_w.append(lw)

    y_prompt = _trunk(x_prompt, mods_for(0, nb_p), None, layer_w)
    y_sample = _trunk(x_sample, mods_for(nb_p, nb_s), None, layer_w)
    return (y_prompt, y_sample)
```

```python
import functools

import numpy as np
import jax
import jax.numpy as jnp
from jax import lax
from jax.experimental import pallas as pl
from jax.experimental.pallas import tpu as pltpu

GRID_W = 64
HEAD_DIM = 64
NA_HEADS = 8
NA_WIN_H = 8
NA_WIN_W = 16
MLA_HEADS = 8
MLA_Q_RANK = 256
MLA_KV_RANK = 128
MLA_NOPE = 64
MLA_ROPE = 32
MLA_V = 64
DIL_PAIRS = ((128, 1), (512, 4), (2048, 16))
DIL_HEADS = 8
N_EXPERTS = 16
EC_CAPACITY = 2
ROPE_THETA = 10000.0
LN_EPS = 1e-5
RMS_EPS = 1e-6
DEPTH = 2
DEEPNORM_ALPHA = (2 * DEPTH) ** 0.25
NA_WIDTH = NA_HEADS * HEAD_DIM
DIL_WIDTH = DIL_HEADS * HEAD_DIM

LANES = 128
BF16_SUBLANES = 16
MASK_VALUE = -1e30
VMEM_LIMIT = 56 * 1024 * 1024

_F32 = jnp.float32
_BF16 = jnp.bfloat16


def _cparams(sem, vmem=None):
    return pltpu.CompilerParams(dimension_semantics=sem, vmem_limit_bytes=vmem)


def _dotf(a, b):
    return jnp.dot(a, b, preferred_element_type=_F32)


def _dot_nt(a, b):
    return lax.dot_general(a, b, (((1,), (1,)), ((), ())), preferred_element_type=_F32)


def _split2(x):
    hi = x.astype(_BF16)
    lo = (x - hi.astype(_F32)).astype(_BF16)
    return hi, lo


def _split3(x):
    hi = x.astype(_BF16)
    r = x - hi.astype(_F32)
    mid = r.astype(_BF16)
    lo = (r - mid.astype(_F32)).astype(_BF16)
    return hi, mid, lo


def _silu(x):
    return x / (1.0 + jnp.exp(-x))


def _rope_rotate(x, half, first_half):
    n = x.shape[-1]
    fwd = pltpu.roll(x, n - half, 1)
    bwd = pltpu.roll(x, half, 1)
    return jnp.where(first_half, fwd, bwd)


def _ada_kernel(c_ref, w_ref, b_ref, o_ref):
    a = _silu(c_ref[...])
    ah, al = _split2(a)
    wh, wl = _split2(w_ref[...])
    o_ref[...] = _dotf(ah, wh) + (_dotf(ah, wl) + _dotf(al, wh)) + b_ref[...]


def _ada_mod(c, w, b):
    L, D, N3 = w.shape
    Bp = c.shape[0]
    tn = min(512, N3)
    return pl.pallas_call(
        _ada_kernel,
        out_shape=jax.ShapeDtypeStruct((L, Bp, N3), _F32),
        grid=(L, N3 // tn),
        in_specs=[
            pl.BlockSpec((Bp, D), lambda l, j: (0, 0)),
            pl.BlockSpec((None, D, tn), lambda l, j: (l, 0, j)),
            pl.BlockSpec((None, 1, tn), lambda l, j: (l, 0, j)),
        ],
        out_specs=pl.BlockSpec((None, Bp, tn), lambda l, j: (l, 0, j)),
        compiler_params=_cparams(("arbitrary", "arbitrary")),
        name="ada_mod",
    )(c, w, b.reshape(L, 1, N3))


INPROJ_CHUNK = 2048
INPROJ_ROWS = 512


def _inproj_kernel(x_ref, shift_ref, scale_ref, w_ref, cos_ref, sin_ref, o_ref, *scratch,
                   dil, tm, rs, rope_cols, half):
    if dil == 1:
        x = x_ref[...]
        cos, sin = cos_ref[...], sin_ref[...]
    else:
        (xs_ref,) = scratch
        nlt = xs_ref.shape[0]

        @pl.when(pl.program_id(2) == 0)
        def _():
            for c in range(nlt):
                xs_ref[c] = x_ref[:, c * LANES:(c + 1) * LANES]

        strides = [pl.ds(pl.program_id(2) * rs + q, tm, stride=dil) for q in range(rs)]
        x = jnp.concatenate(
            [jnp.concatenate([xs_ref[c, rows, :] for c in range(nlt)], axis=1) for rows in strides], axis=0)
        cos = jnp.concatenate([cos_ref[rows, :] for rows in strides], axis=0)
        sin = jnp.concatenate([sin_ref[rows, :] for rows in strides], axis=0)
    h = x * (1.0 + scale_ref[...]) + shift_ref[...]
    acc = _dotf(h.astype(_BF16), w_ref[...])
    n = acc.shape[1]

    def store(cols, blk):
        blk = blk.astype(o_ref.dtype)
        for q in range(rs):
            o_ref[q, :, cols] = blk[q * tm:(q + 1) * tm]

    if rope_cols:
        lane = lax.broadcasted_iota(jnp.int32, cos.shape, 1)
        first = (lane % (2 * half)) < half
        for j in range(rope_cols // LANES):
            blk = acc[:, j * LANES:(j + 1) * LANES]
            store(slice(j * LANES, (j + 1) * LANES), blk * cos + _rope_rotate(blk, half, first) * sin)
    store(slice(rope_cols, n), acc[:, rope_cols:])


def _inproj(x, shift, scale, w, cos, sin, *, dil, rope_cols, half, tm):
    B, S, D = x.shape
    N = w.shape[1]
    L = S // dil
    tm = min(tm, L, INPROJ_CHUNK // dil)
    chunk = tm * dil
    rs = max(1, min(dil, INPROJ_ROWS // tm))
    kern = functools.partial(_inproj_kernel, dil=dil, tm=tm, rs=rs, rope_cols=rope_cols, half=half)
    return pl.pallas_call(
        kern,
        out_shape=jax.ShapeDtypeStruct((B, dil, L, N), _BF16),
        grid=(B, L // tm, dil // rs),
        in_specs=[
            pl.BlockSpec((None, chunk, D), lambda b, j, r: (b, j, 0)),
            pl.BlockSpec((None, 1, D), lambda b, j, r: (b, 0, 0)),
            pl.BlockSpec((None, 1, D), lambda b, j, r: (b, 0, 0)),
            pl.BlockSpec((D, N), lambda b, j, r: (0, 0)),
            pl.BlockSpec((chunk, LANES), lambda b, j, r: (j, 0)),
            pl.BlockSpec((chunk, LANES), lambda b, j, r: (j, 0)),
        ],
        out_specs=pl.BlockSpec((None, rs, tm, N), lambda b, j, r: (b, r, j, 0)),
        scratch_shapes=[pltpu.VMEM((D // LANES, chunk, LANES), _F32)] if dil > 1 else [],
        compiler_params=_cparams(("arbitrary",) * 3, VMEM_LIMIT),
        name="inproj",
    )(x, shift, scale, w, cos, sin)


def _na_kernel(q_ref, k_ref, v_ref, bias_ref, o_ref, *, rows):
    nk = NA_WIN_H * GRID_W
    lane = lax.broadcasted_iota(jnp.int32, (GRID_W, LANES), 1)
    low = lane < HEAD_DIM

    def row_step(r, carry):
        r0 = jnp.clip(r - NA_WIN_H // 2, 0, rows - NA_WIN_H)
        case = r - r0
        q = q_ref[pl.ds(pl.multiple_of(r * GRID_W, GRID_W), GRID_W), :]
        kw = k_ref[pl.ds(pl.multiple_of(r0 * GRID_W, GRID_W), nk), :]
        vw = v_ref[pl.ds(pl.multiple_of(r0 * GRID_W, GRID_W), nk), :]
        zero = jnp.zeros_like(q)
        q2 = jnp.concatenate([jnp.where(low, q, zero), jnp.where(low, zero, q)], axis=0)
        q2 = q2 * jnp.asarray(HEAD_DIM ** -0.5, q.dtype)
        s = _dot_nt(q2, kw) + bias_ref[:, case].reshape(2 * GRID_W, nk)
        m = jnp.max(s, axis=1, keepdims=True)
        p = jnp.exp(s - m)
        l = jnp.sum(p, axis=1, keepdims=True)
        o2 = _dotf(p.astype(_BF16), vw) / l
        o = jnp.where(low, o2[:GRID_W], o2[GRID_W:])
        o_ref[pl.ds(pl.multiple_of(r * GRID_W, GRID_W), GRID_W), :] = o.astype(o_ref.dtype)
        return carry

    lax.fori_loop(0, rows, row_step, 0, unroll=4)


def _na_bias_table(rpb):
    kh, kw = NA_WIN_H, NA_WIN_W
    H = rpb.shape[0]
    W = GRID_W
    P = 2 * W
    lead = (W - 1) - (kw - 1)
    row = jnp.pad(rpb.astype(_F32), ((0, 0), (0, 0), (lead, P - lead - (2 * kw - 1))))
    flat = jnp.tile(row, (1, 1, W))[:, :, :W * (P - 1)]
    toep = flat.reshape(H, 2 * kh - 1, W, P - 1)[:, :, :, W - 1:]
    cols = np.arange(W)
    c0 = np.clip(cols - kw // 2, 0, W - kw)
    valid = (cols[None, :] >= c0[:, None]) & (cols[None, :] < c0[:, None] + kw)
    toep = jnp.where(valid[None, None], toep, MASK_VALUE)
    per_case = [jnp.transpose(toep[:, kh - 1 - case:2 * kh - 1 - case], (0, 2, 1, 3)) for case in range(kh)]
    return jnp.stack(per_case, axis=1).reshape(H, kh, W, kh * W)


def _na_attention(proj, bias_tab):
    B, S, _ = proj.shape
    rows = S // GRID_W
    assert rows >= NA_WIN_H
    npair = NA_HEADS // 2
    kern = functools.partial(_na_kernel, rows=rows)
    return pl.pallas_call(
        kern,
        out_shape=jax.ShapeDtypeStruct((B, S, NA_WIDTH), _BF16),
        grid=(B, npair),
        in_specs=[
            pl.BlockSpec((None, S, LANES), lambda b, p: (b, 0, p)),
            pl.BlockSpec((None, S, LANES), lambda b, p: (b, 0, npair + p)),
            pl.BlockSpec((None, S, LANES), lambda b, p: (b, 0, 2 * npair + p)),
            pl.BlockSpec((2, NA_WIN_H, GRID_W, NA_WIN_H * GRID_W), lambda b, p: (p, 0, 0, 0)),
        ],
        out_specs=pl.BlockSpec((None, S, LANES), lambda b, p: (b, 0, p)),
        compiler_params=_cparams(("arbitrary", "arbitrary"), VMEM_LIMIT),
        name="na_attention",
    )(proj, proj, proj, bias_tab)


def _rms(x, g):
    xf = x.astype(_F32)
    return xf * lax.rsqrt(jnp.mean(xf * xf, axis=-1, keepdims=True) + RMS_EPS) * g


def _mla_prep_kernel(lat_ref, qn_ref, kn_ref, wq_ref, wk_ref, wv_ref, cos_ref, sin_ref,
                     q_ref, k_ref, v_ref):
    lat = lat_ref[...]
    qn = _rms(lat[:, :MLA_Q_RANK], qn_ref[...]).astype(_BF16)
    kvn = _rms(lat[:, MLA_Q_RANK:MLA_Q_RANK + MLA_KV_RANK], kn_ref[...]).astype(_BF16)
    kr = lat[:, MLA_Q_RANK + MLA_KV_RANK:].astype(_F32)
    cos = cos_ref[...]
    sin = sin_ref[...]
    lane = lax.broadcasted_iota(jnp.int32, cos.shape, 1)
    half = MLA_ROPE // 2
    first = lane < MLA_NOPE + half
    scale = (MLA_NOPE + MLA_ROPE) ** -0.5
    kpe = kr * cos + _rope_rotate(kr, half, first) * sin
    q_all = _dotf(qn, wq_ref[...])
    k_all = _dotf(kvn, wk_ref[...])
    v_all = _dotf(kvn, wv_ref[...])
    ones_hi = jnp.where(lane >= MLA_V, 1.0, 0.0)
    for h in range(MLA_HEADS):
        sl = slice(h * LANES, (h + 1) * LANES)
        qh = q_all[:, sl]
        qh = (qh * cos + _rope_rotate(qh, half, first) * sin) * scale
        q_ref[:, sl] = qh.astype(q_ref.dtype)
        k_ref[:, sl] = (k_all[:, sl] + kpe).astype(k_ref.dtype)
        v_ref[:, sl] = (v_all[:, sl] + ones_hi).astype(v_ref.dtype)


def _mla_prep(proj, col_block, qn, kn, wq, wk, wv, cos, sin, tm):
    B, S, _ = proj.shape
    wlat = MLA_Q_RANK + MLA_KV_RANK + LANES
    W = MLA_HEADS * LANES
    tm = min(tm, S)
    out = jax.ShapeDtypeStruct((B, S, W), _BF16)
    full = lambda shp: pl.BlockSpec(shp, lambda b, i: (0,) * len(shp))
    return pl.pallas_call(
        _mla_prep_kernel,
        out_shape=(out, out, out),
        grid=(B, S // tm),
        in_specs=[
            pl.BlockSpec((None, tm, wlat), lambda b, i: (b, i, col_block)),
            full((1, MLA_Q_RANK)), full((1, MLA_KV_RANK)),
            full(wq.shape), full(wk.shape), full(wv.shape),
            pl.BlockSpec((tm, LANES), lambda b, i: (i, 0)),
            pl.BlockSpec((tm, LANES), lambda b, i: (i, 0)),
        ],
        out_specs=(pl.BlockSpec((None, tm, W), lambda b, i: (b, i, 0)),) * 3,
        compiler_params=_cparams(("arbitrary", "arbitrary"), VMEM_LIMIT),
        name="mla_prep",
    )(proj, qn, kn, wq, wk, wv, cos, sin)


def _flash_kernel(q_ref, k_ref, v_ref, o_ref, s_ref, m_ref, acc_ref, *, tq, tk):
    S = k_ref.shape[0]
    npair = S // (2 * tk)
    ntile = S // tq

    def reset():
        m_ref[...] = jnp.full(m_ref.shape, MASK_VALUE, _F32)
        acc_ref[...] = jnp.zeros(acc_ref.shape, _F32)

    def scores(tile, i):
        q = q_ref[pl.ds(pl.multiple_of(tile * tq, tq), tq), :]
        return _dot_nt(q, k_ref[pl.ds(pl.multiple_of(i * tk, tk), tk), :])

    def softmax_update(i, slot):
        s = s_ref[slot]
        chunks = [s[:, c * LANES:(c + 1) * LANES] for c in range(tk // LANES)]
        mx = functools.reduce(jnp.maximum, chunks)
        m_old = m_ref[...]
        m_new = jnp.maximum(m_old, jnp.max(mx, axis=1, keepdims=True))
        p = jnp.concatenate([jnp.exp((c - m_new).astype(_BF16)) for c in chunks], axis=1)
        pv = _dotf(p, v_ref[pl.ds(pl.multiple_of(i * tk, tk), tk), :])
        acc_ref[...] = acc_ref[...] * jnp.exp(m_old - m_new) + pv
        m_ref[...] = m_new

    reset()
    s_ref[0] = scores(0, 0)

    def kv_pair(t, carry):
        tile = lax.div(t, npair)
        i = 2 * lax.rem(t, npair)
        last = i == 2 * (npair - 1)
        s_ref[1] = scores(tile, i + 1)
        softmax_update(i, 0)
        s_ref[0] = scores(jnp.where(last, jnp.minimum(tile + 1, ntile - 1), tile), jnp.where(last, 0, i + 2))
        softmax_update(i + 1, 1)

        @pl.when(last)
        def _():
            acc = acc_ref[...]
            lane = lax.broadcasted_iota(jnp.int32, acc.shape, 1)
            den = jnp.where(lane < MLA_V, pltpu.roll(acc, MLA_V, 1), acc)
            o_ref[pl.ds(pl.multiple_of(tile * tq, tq), tq), :] = (acc / den).astype(o_ref.dtype)
            reset()

        return carry

    lax.fori_loop(0, ntile * npair, kv_pair, 0)


def _mla_flash(q, k, v, tq, tk):
    B, S, W = q.shape
    tq = min(tq, S)
    tk = min(tk, S // 2)
    assert S % (2 * tk) == 0 and S % tq == 0
    kern = functools.partial(_flash_kernel, tq=tq, tk=tk)
    spec = lambda: pl.BlockSpec((None, S, LANES), lambda b, h: (b, 0, h))
    return pl.pallas_call(
        kern,
        out_shape=jax.ShapeDtypeStruct((B, S, W), _BF16),
        grid=(B, MLA_HEADS),
        in_specs=[spec(), spec(), spec()],
        out_specs=spec(),
        scratch_shapes=[pltpu.VMEM((2, tq, tk), _F32), pltpu.VMEM((tq, LANES), _F32),
                        pltpu.VMEM((tq, LANES), _F32)],
        compiler_params=_cparams(("arbitrary",) * 2, VMEM_LIMIT),
        name="mla_flash",
    )(q, k, v)


DIL_QBLK = 128
DIL_HALF = 64
DIL_KWIN = DIL_QBLK + 2 * DIL_HALF
DIL_ALL_PAIRS_MAX_LEN = 2048


def _dil_mask_table():
    qi = np.arange(2 * DIL_QBLK)[None, :, None] % DIL_QBLK
    ki = np.arange(DIL_KWIN)[None, None, :]
    back = (np.arange(3) * DIL_HALF)[:, None, None]
    valid = np.abs(ki - back - qi) <= DIL_HALF
    return jnp.asarray(np.where(valid, 0.0, MASK_VALUE), _F32)


def _dil_kernel(q_ref, k_ref, v_ref, mask_ref, o_ref, lse_ref, *, L):
    lane = lax.broadcasted_iota(jnp.int32, (DIL_QBLK, LANES), 1)
    low = lane < HEAD_DIM

    def blk_step(i, carry, cols):
        q0 = pl.multiple_of(i * DIL_QBLK, DIL_QBLK)
        start = pl.multiple_of(jnp.clip(q0 - DIL_HALF, 0, L - DIL_KWIN), DIL_HALF)
        q = q_ref[pl.ds(q0, DIL_QBLK), cols]
        kw = k_ref[pl.ds(start, DIL_KWIN), cols]
        vw = v_ref[pl.ds(start, DIL_KWIN), cols]
        zero = jnp.zeros_like(q)
        q2 = jnp.concatenate([jnp.where(low, q, zero), jnp.where(low, zero, q)], axis=0)
        q2 = q2 * jnp.asarray(HEAD_DIM ** -0.5, q.dtype)
        s = _dot_nt(q2, kw) + mask_ref[lax.div(q0 - start, DIL_HALF)]
        m = jnp.max(s, axis=1, keepdims=True)
        p = jnp.exp(s - m)
        l = jnp.sum(p, axis=1, keepdims=True)
        o2 = _dotf(p.astype(_BF16), vw) / l
        lse2 = m + jnp.log(l)
        o_ref[pl.ds(q0, DIL_QBLK), cols] = jnp.where(low, o2[:DIL_QBLK], o2[DIL_QBLK:]).astype(o_ref.dtype)
        lse_ref[pl.ds(q0, DIL_QBLK), cols] = jnp.where(low, lse2[:DIL_QBLK], lse2[DIL_QBLK:])
        return carry

    nblk = L // DIL_QBLK
    for pair in range(q_ref.shape[1] // LANES):
        step = functools.partial(blk_step, cols=slice(pair * LANES, (pair + 1) * LANES))
        lax.fori_loop(0, nblk, step, 0, unroll=4 if nblk % 4 == 0 else 2)


def _dil_attention(qkv):
    B, d, L, _ = qkv.shape
    assert L >= DIL_KWIN and L % DIL_QBLK == 0
    npair = DIL_HEADS // 2
    pp = npair if L <= DIL_ALL_PAIRS_MAX_LEN else 1
    ng = npair // pp
    w = pp * LANES
    kern = functools.partial(_dil_kernel, L=L)
    return pl.pallas_call(
        kern,
        out_shape=(jax.ShapeDtypeStruct((B, d, L, DIL_WIDTH), _BF16),
                   jax.ShapeDtypeStruct((B, d, L, DIL_WIDTH), _F32)),
        grid=(B, d, ng),
        in_specs=[
            pl.BlockSpec((None, None, L, w), lambda b, r, p: (b, r, 0, p)),
            pl.BlockSpec((None, None, L, w), lambda b, r, p: (b, r, 0, ng + p)),
            pl.BlockSpec((None, None, L, w), lambda b, r, p: (b, r, 0, 2 * ng + p)),
            pl.BlockSpec((3, 2 * DIL_QBLK, DIL_KWIN), lambda b, r, p: (0, 0, 0)),
        ],
        out_specs=(pl.BlockSpec((None, None, L, w), lambda b, r, p: (b, r, 0, p)),) * 2,
        compiler_params=_cparams(("arbitrary",) * 3, VMEM_LIMIT),
        name="dil_attention",
    )(qkv, qkv, qkv, _dil_mask_table())


def _post_mixer(out, x_ref, gate_ref, g_ref, b_ref, shift2_ref, scale2_ref, wr_ref,
                xo_ref, h_ref, aff_ref):
    y = DEEPNORM_ALPHA * x_ref[...] + gate_ref[...] * out
    mu = jnp.mean(y, axis=-1, keepdims=True)
    yc = y - mu
    var = jnp.mean(yc * yc, axis=-1, keepdims=True)
    xn = yc * lax.rsqrt(var + LN_EPS) * g_ref[...] + b_ref[...]
    xo_ref[...] = xn
    h = xn * (1.0 + scale2_ref[...]) + shift2_ref[...]
    h_ref[...] = h
    hh, hl = _split2(h)
    wh, wl = _split2(wr_ref[...])
    logits = _dot_nt(wh, hh) + (_dot_nt(wh, hl) + _dot_nt(wl, hh))
    mx = jnp.max(logits, axis=0, keepdims=True)
    ex = jnp.exp(logits - mx)
    aff_ref[...] = ex / jnp.sum(ex, axis=0, keepdims=True)


def _outproj_ab_kernel(a1_ref, a2_ref, w1_ref, w2_ref, *rest):
    out = _dotf(a1_ref[...], w1_ref[...]) + _dotf(a2_ref[...], w2_ref[...])
    _post_mixer(out, *rest)


def _outproj_dil_kernel(o1, l1, o2, l2, o3, l3, w_ref, *rest):
    *rest, obuf, lbuf = rest
    nlt = obuf.shape[1]
    for g, (o_ref, l_ref) in enumerate(((o1, l1), (o2, l2), (o3, l3))):
        d, n, _ = o_ref.shape
        for r in range(d):
            rows = slice(None) if d == 1 else pl.ds(r, n, stride=d)
            o_r = o_ref[r].astype(_F32)
            l_r = l_ref[r]
            for c in range(nlt):
                obuf[g, c, rows, :] = o_r[:, c * LANES:(c + 1) * LANES]
                lbuf[g, c, rows, :] = l_r[:, c * LANES:(c + 1) * LANES]
    mixed = []
    for c in range(nlt):
        ls = [lbuf[g, c] for g in range(3)]
        mx = jnp.maximum(jnp.maximum(ls[0], ls[1]), ls[2])
        ws = [jnp.exp(l - mx) for l in ls]
        den = ws[0] + ws[1] + ws[2]
        mixed.append(sum((w / den) * obuf[g, c] for g, w in enumerate(ws)))
    out = _dotf(jnp.concatenate(mixed, axis=1).astype(_BF16), w_ref[...])
    _post_mixer(out, *rest)


def _outproj(kind, acts, weights, x, gate, ln_g, ln_b, shift2, scale2, wr_t, tm):
    B, S, D = x.shape
    E = wr_t.shape[0]
    tm = min(tm, S)
    nt = S // tm
    tok = lambda w: pl.BlockSpec((None, tm, w), lambda b, i: (b, i, 0))
    per_b = pl.BlockSpec((None, 1, D), lambda b, i: (b, 0, 0))
    full = lambda shp: pl.BlockSpec(shp, lambda b, i: (0,) * len(shp))
    if kind == "ab":
        kern, scratch = _outproj_ab_kernel, []
        in_specs = [tok(a.shape[-1]) for a in acts]
    else:
        kern = _outproj_dil_kernel
        in_specs = [pl.BlockSpec((None, a.shape[1], tm // a.shape[1], a.shape[3]), lambda b, i: (b, 0, i, 0))
                    for a in acts]
        scratch = [pltpu.VMEM((len(acts) // 2, acts[0].shape[3] // LANES, tm, LANES), _F32)] * 2
    in_specs += [full(w.shape) for w in weights]
    in_specs += [tok(D), per_b, full((1, D)), full((1, D)), per_b, per_b, full(wr_t.shape)]
    return pl.pallas_call(
        kern,
        out_shape=(jax.ShapeDtypeStruct((B, S, D), _F32),
                   jax.ShapeDtypeStruct((B, S, D), _F32),
                   jax.ShapeDtypeStruct((E, B * S), _F32)),
        grid=(B, nt),
        in_specs=in_specs,
        out_specs=(tok(D), tok(D), pl.BlockSpec((E, tm), lambda b, i: (0, b * nt + i))),
        scratch_shapes=scratch,
        compiler_params=_cparams(("arbitrary", "arbitrary"), VMEM_LIMIT),
        name="outproj_" + kind,
    )(*acts, *weights, x, gate, ln_g, ln_b, shift2, scale2, wr_t)


def _excl_cumsum_rows(x_bf, R):
    li = lax.broadcasted_iota(jnp.int32, (LANES, LANES), 0)
    lj = lax.broadcasted_iota(jnp.int32, (LANES, LANES), 1)
    upper = jnp.where(li < lj, 1.0, 0.0).astype(_BF16)
    ones = jnp.ones((LANES, LANES), _BF16)
    within = _dotf(x_bf, upper)
    rowtot = _dotf(x_bf, ones)
    ri = lax.broadcasted_iota(jnp.int32, (R, R), 0)
    rj = lax.broadcasted_iota(jnp.int32, (R, R), 1)
    lower = jnp.where(rj < ri, 1.0, 0.0).astype(_BF16)
    return within + _dotf(lower, rowtot.astype(_BF16))


def _select_kernel(aff_ref, cs_ref, cnt_ref, *, cap):
    aff = aff_ref[...]
    R = aff.shape[0]
    bits = pltpu.bitcast(aff, jnp.int32)
    capf = jnp.float32(cap)

    def count_ge(t):
        return jnp.sum(jnp.where(bits >= t, 1.0, 0.0))

    def search(_, lohi):
        lo, hi = lohi
        mid = lo + lax.shift_right_logical(hi - lo, 1)
        ok = count_ge(mid) >= capf
        return jnp.where(ok, mid, lo), jnp.where(ok, hi, mid)

    lo, _ = lax.fori_loop(0, 31, search, (jnp.int32(0), jnp.int32(0x7F800000)))
    gt = bits > lo
    eq = bits == lo
    need = capf - jnp.sum(jnp.where(gt, 1.0, 0.0))
    eq_rank = _excl_cumsum_rows(jnp.where(eq, 1.0, 0.0).astype(_BF16), R)
    sel = jnp.logical_or(gt, jnp.logical_and(eq, eq_rank < need))
    cs = _excl_cumsum_rows(jnp.where(sel, 1.0, 0.0).astype(_BF16), R)
    cs_ref[...] = jnp.where(sel, cs, -1.0)
    cnt_ref[...] = cs.astype(jnp.int32)


def _moe_select(aff_t, cap):
    E, N = aff_t.shape
    R = N // LANES
    kern = functools.partial(_select_kernel, cap=cap)
    spec = pl.BlockSpec((None, R, LANES), lambda e: (e, 0, 0))
    slot, before = pl.pallas_call(
        kern,
        out_shape=(jax.ShapeDtypeStruct((E, R, LANES), _F32), jax.ShapeDtypeStruct((E, R, LANES), jnp.int32)),
        grid=(E,),
        in_specs=[spec],
        out_specs=(spec, spec),
        compiler_params=_cparams(("arbitrary",), VMEM_LIMIT),
        name="moe_select",
    )(aff_t.reshape(E, R, LANES))
    return slot.reshape(E, N), before.reshape(E, N)


COMPACT_TILE = 512


def _compact_kernel(slot_ref, aff_ref, idx_ref, gate_ref):
    k = pl.program_id(1)
    slot = slot_ref[...]
    R = slot.shape[0]
    T = COMPACT_TILE
    m_bf = jnp.where(slot >= 0.0, 1.0, 0.0).astype(_BF16)
    ones8 = jnp.ones((8, LANES), _BF16)
    rowtot = _dot_nt(ones8, m_bf)
    ri = lax.broadcasted_iota(jnp.int32, (R, R), 0)
    rj = lax.broadcasted_iota(jnp.int32, (R, R), 1)
    incl = _dotf(rowtot.astype(_BF16), jnp.where(ri <= rj, 1.0, 0.0).astype(_BF16))
    excl = incl - rowtot
    c = (k * T + lax.broadcasted_iota(jnp.int32, (T, R), 0)).astype(_F32)
    oh = jnp.logical_and(excl[0:1, :] <= c, c < incl[0:1, :])
    ohb = jnp.where(oh, 1.0, 0.0).astype(_BF16)
    li = lax.broadcasted_iota(jnp.int32, (LANES, LANES), 0)
    lj = lax.broadcasted_iota(jnp.int32, (LANES, LANES), 1)
    loc_incl = _dotf(m_bf, jnp.where(li <= lj, 1.0, 0.0).astype(_BF16))
    a_hi, a_mid, a_lo = _split3(aff_ref[...])
    g_cs = _dotf(ohb, loc_incl.astype(_BF16))
    g_aff = _dotf(ohb, a_hi) + (_dotf(ohb, a_mid) + _dotf(ohb, a_lo))
    excl_sel = jnp.sum(jnp.where(oh, excl[0:1, :], 0.0), axis=1, keepdims=True)
    row_sel = jnp.sum(jnp.where(oh, rj[0:1, :].astype(_F32), 0.0), axis=1, keepdims=True)
    target = c[:, 0:1] - excl_sel + 1.0
    lane_sel = jnp.sum(jnp.where(g_cs < target, 1.0, 0.0), axis=1, keepdims=True)
    lane_f = lax.broadcasted_iota(jnp.int32, (T, LANES), 1).astype(_F32)
    gate = jnp.sum(jnp.where(lane_f == lane_sel, g_aff, 0.0), axis=1, keepdims=True)
    idx_ref[...] = jnp.broadcast_to(row_sel * LANES + lane_sel, (T, LANES)).astype(jnp.int32)
    gate_ref[...] = jnp.broadcast_to(gate, (T, LANES))


def _moe_compact(slot, aff_t, cap):
    E, N = slot.shape
    R = N // LANES
    T = COMPACT_TILE
    nt = cap // T
    spec_in = pl.BlockSpec((None, R, LANES), lambda e, k: (e, 0, 0))
    spec_out = pl.BlockSpec((T, LANES), lambda e, k: (e * nt + k, 0))
    return pl.pallas_call(
        _compact_kernel,
        out_shape=(jax.ShapeDtypeStruct((E * cap, LANES), jnp.int32),
                   jax.ShapeDtypeStruct((E * cap, LANES), _F32)),
        grid=(E, nt),
        in_specs=[spec_in, spec_in],
        out_specs=(spec_out, spec_out),
        compiler_params=_cparams(("arbitrary", "arbitrary"), VMEM_LIMIT),
        name="moe_compact",
    )(slot.reshape(E, R, LANES), aff_t.reshape(E, R, LANES))


def _row_copy(h_hbm, xbuf, sem, src_row, dst_row):
    return pltpu.make_async_copy(h_hbm.at[pl.ds(src_row, 1)], xbuf.at[pl.ds(dst_row, 1)], sem)


def _ffn_kernel(idx_ref, idx_next_ref, gate_ref, wg_ref, wu_ref, wd_ref, h_hbm, o_ref, xbuf, xb_ref, sems,
                *, tm, nsteps):
    step = pl.program_id(0)
    slot = lax.rem(step, 2)

    def drain(slot):
        def wait(i, carry):
            _row_copy(h_hbm, xbuf.at[slot], sems.at[slot], 0, i).wait()
            return carry

        lax.fori_loop(0, tm, wait, 0, unroll=8)

    @pl.when(step == 0)
    def _():
        def issue(i, carry):
            _row_copy(h_hbm, xbuf.at[0], sems.at[0], idx_ref[i], i).start()
            return carry

        lax.fori_loop(0, tm, issue, 0, unroll=8)

    drain(slot)
    xb_ref[...] = xbuf[slot].astype(_BF16)
    for i in range(tm):
        _row_copy(h_hbm, xbuf.at[1 - slot], sems.at[1 - slot], idx_next_ref[i], i).start()
    x = xb_ref[...]
    g = _dotf(x, wg_ref[...])
    u = _dotf(x, wu_ref[...])
    mid = (_silu(g) * u).astype(_BF16)
    y = _dotf(mid, wd_ref[...])
    gate = gate_ref[...]
    for j in range(y.shape[1] // LANES):
        o_ref[:, j * LANES:(j + 1) * LANES] = (y[:, j * LANES:(j + 1) * LANES] * gate).astype(o_ref.dtype)

    @pl.when(step == nsteps - 1)
    def _():
        drain(1 - slot)


def _moe_ffn(idx, gate, h, wg, wu, wd, cap, tm):
    E, D, F = wg.shape
    tm = min(tm, cap)
    nt = cap // tm
    nsteps = E * nt
    kern = functools.partial(_ffn_kernel, tm=tm, nsteps=nsteps)
    return pl.pallas_call(
        kern,
        out_shape=jax.ShapeDtypeStruct((E * cap, D), _BF16),
        grid=(nsteps,),
        in_specs=[
            pl.BlockSpec((tm,), lambda s: (s,), memory_space=pltpu.SMEM),
            pl.BlockSpec((tm,), lambda s: (jnp.minimum(s + 1, nsteps - 1),), memory_space=pltpu.SMEM),
            pl.BlockSpec((tm, LANES), lambda s: (s, 0)),
            pl.BlockSpec((None, D, F), lambda s: (s // nt, 0, 0), pipeline_mode=pl.Buffered(1)),
            pl.BlockSpec((None, D, F), lambda s: (s // nt, 0, 0), pipeline_mode=pl.Buffered(1)),
            pl.BlockSpec((None, F, D), lambda s: (s // nt, 0, 0), pipeline_mode=pl.Buffered(1)),
            pl.BlockSpec(memory_space=pl.ANY),
        ],
        out_specs=pl.BlockSpec((tm, D), lambda s: (s, 0)),
        scratch_shapes=[pltpu.VMEM((2, tm, D), _F32), pltpu.VMEM((tm, D), _BF16),
                        pltpu.SemaphoreType.DMA((2,))],
        compiler_params=_cparams(("arbitrary",), VMEM_LIMIT),
        name="moe_ffn",
    )(idx, idx, gate, wg, wu, wd, h)


COMBINE_TOK = 256
COMBINE_WIN = 128


def _combine_kernel(bstart_ref, slot_ref, x_ref, gate_ref, g_ref, b_ref, ye_hbm, o_ref,
                    ybuf, xtra, sems, xsem, acc_ref, *, cap, n_exp, nblk):
    blk = pl.program_id(0)
    par = lax.rem(blk, 2)
    W = COMBINE_WIN

    def aligned_start(b, e):
        s0 = bstart_ref[b * n_exp + e]
        return lax.shift_left(lax.shift_right_logical(s0, 4), 4)

    def window(b, e, j):
        row = e * cap + jnp.minimum(aligned_start(b, e) + j * W, cap - W)
        return pl.multiple_of(row, BF16_SUBLANES)

    def first_windows(b, par):
        return [pltpu.make_async_copy(ye_hbm.at[pl.ds(window(b, e, 0), W)],
                                      ybuf.at[par, pl.ds(e * W, W)], sems.at[par]) for e in range(n_exp)]

    @pl.when(blk == 0)
    def _():
        for cp in first_windows(0, 0):
            cp.start()

    @pl.when(blk + 1 < nblk)
    def _():
        for cp in first_windows(blk + 1, 1 - par):
            cp.start()

    for cp in first_windows(blk, par):
        cp.wait()

    slots = slot_ref[...]
    lane = lax.broadcasted_iota(jnp.int32, (COMBINE_TOK, W), 1)

    def onehot(e, row0, first_slot):
        slot_e = slots[:, e:e + 1].astype(jnp.int32)
        local = jnp.where(slot_e >= first_slot, slot_e - (row0 - e * cap), -1)
        return jnp.where(local == lane, 1.0, 0.0).astype(_BF16)

    placement = jnp.concatenate([onehot(e, window(blk, e, 0), 0) for e in range(n_exp)], axis=1)
    acc_ref[...] = _dotf(placement, ybuf[par])

    for e in range(n_exp):
        al = aligned_start(blk, e)
        s1 = bstart_ref[(blk + 1) * n_exp + e]
        nwin = lax.div(s1 - al + (W - 1), W)

        def extra(j, carry, e=e, al=al):
            r0 = window(blk, e, j)
            cp = pltpu.make_async_copy(ye_hbm.at[pl.ds(r0, W)], xtra, xsem)
            cp.start()
            cp.wait()
            acc_ref[...] += _dotf(onehot(e, r0, al + j * W), xtra[...])
            return carry

        lax.fori_loop(1, nwin, extra, 0)

    y = DEEPNORM_ALPHA * x_ref[...] + gate_ref[...] * acc_ref[...]
    mu = jnp.mean(y, axis=-1, keepdims=True)
    yc = y - mu
    var = jnp.mean(yc * yc, axis=-1, keepdims=True)
    o_ref[...] = yc * lax.rsqrt(var + LN_EPS) * g_ref[...] + b_ref[...]


def _moe_combine(bstart, slot_t, x, gate, ln_g, ln_b, ye, cap):
    B, S, D = x.shape
    N = B * S
    E = slot_t.shape[1]
    T = COMBINE_TOK
    per_seq = S // T
    kern = functools.partial(_combine_kernel, cap=cap, n_exp=E, nblk=N // T)
    out = pl.pallas_call(
        kern,
        out_shape=jax.ShapeDtypeStruct((N, D), _F32),
        grid_spec=pltpu.PrefetchScalarGridSpec(
            num_scalar_prefetch=1,
            grid=(N // T,),
            in_specs=[
                pl.BlockSpec((T, E), lambda i, bs: (i, 0)),
                pl.BlockSpec((T, D), lambda i, bs: (i, 0)),
                pl.BlockSpec((None, 1, D), lambda i, bs: (i // per_seq, 0, 0)),
                pl.BlockSpec((1, D), lambda i, bs: (0, 0)),
                pl.BlockSpec((1, D), lambda i, bs: (0, 0)),
                pl.BlockSpec(memory_space=pl.ANY),
            ],
            out_specs=pl.BlockSpec((T, D), lambda i, bs: (i, 0)),
            scratch_shapes=[
                pltpu.VMEM((2, E * COMBINE_WIN, D), _BF16),
                pltpu.VMEM((COMBINE_WIN, D), _BF16),
                pltpu.SemaphoreType.DMA((2,)),
                pltpu.SemaphoreType.DMA,
                pltpu.VMEM((T, D), _F32),
            ],
        ),
        compiler_params=_cparams(("arbitrary",), VMEM_LIMIT),
        name="moe_combine",
    )(bstart, slot_t, x.reshape(N, D), gate, ln_g, ln_b, ye)
    return out.reshape(B, S, D)


def _expert_choice(x1, h2, aff_t, gate, ln_g, ln_b, wg, wu, wd, ffn_tm):
    B, S, D = x1.shape
    N = B * S
    E = aff_t.shape[0]
    cap = EC_CAPACITY * N // E
    slot, before = _moe_select(aff_t, cap)
    idx_rep, gate_rep = _moe_compact(slot, aff_t, cap)
    ye = _moe_ffn(idx_rep[:, 0], gate_rep, h2.reshape(N, D), wg, wu, wd, cap, ffn_tm)
    bstart = jnp.concatenate([before[:, ::COMBINE_TOK], jnp.full((E, 1), cap, jnp.int32)], axis=1)
    return _moe_combine(bstart.T.reshape(-1), slot.T, x1, gate, ln_g, ln_b, ye, cap)


def _rope_np(n, dim):
    inv = 1.0 / (ROPE_THETA ** (jnp.arange(0, dim, 2, dtype=_F32) / dim))
    ang = jnp.arange(n, dtype=_F32)[:, None] * inv[None, :]
    return jnp.cos(ang), jnp.sin(ang)


def _head_rope_tables(S):
    cos, sin = _rope_np(S, HEAD_DIM)
    c = jnp.tile(cos, (1, LANES // (HEAD_DIM // 2)))
    s = jnp.tile(jnp.concatenate([-sin, sin], axis=1), (1, LANES // HEAD_DIM))
    return c, s


def _mla_rope_tables(S):
    cos, sin = _rope_np(S, MLA_ROPE)
    pad = LANES - MLA_NOPE - MLA_ROPE
    c = jnp.concatenate([jnp.ones((S, MLA_NOPE), _F32), cos, cos, jnp.zeros((S, pad), _F32)], axis=1)
    s = jnp.concatenate([jnp.zeros((S, MLA_NOPE), _F32), -sin, sin, jnp.zeros((S, pad), _F32)], axis=1)
    return c, s


def _prep_ab_weights(w_in, w_q_up, w_kv_up, w_out):
    D = w_in.shape[0]
    s1 = 3 * NA_WIDTH
    s3 = s1 + MLA_Q_RANK + MLA_KV_RANK
    pad = LANES - MLA_NOPE - MLA_ROPE
    krope = jnp.concatenate([jnp.zeros((D, MLA_NOPE), _F32), w_in[:, s3:], jnp.zeros((D, pad), _F32)], axis=1)
    w_in_p = jnp.concatenate([w_in[:, :s3], krope], axis=1).astype(_BF16)
    wq = w_q_up.reshape(MLA_Q_RANK, MLA_HEADS, MLA_NOPE + MLA_ROPE)
    wq = jnp.pad(wq, ((0, 0), (0, 0), (0, pad))).reshape(MLA_Q_RANK, MLA_HEADS * LANES).astype(_BF16)
    wkv = w_kv_up.reshape(MLA_KV_RANK, MLA_HEADS, MLA_NOPE + MLA_V)
    wk = jnp.pad(wkv[:, :, :MLA_NOPE], ((0, 0), (0, 0), (0, LANES - MLA_NOPE)))
    wv = jnp.pad(wkv[:, :, MLA_NOPE:], ((0, 0), (0, 0), (0, LANES - MLA_V)))
    wk = wk.reshape(MLA_KV_RANK, MLA_HEADS * LANES).astype(_BF16)
    wv = wv.reshape(MLA_KV_RANK, MLA_HEADS * LANES).astype(_BF16)
    w_out_na = w_out[:NA_WIDTH].astype(_BF16)
    wo = w_out[NA_WIDTH:].reshape(MLA_HEADS, MLA_V, D)
    w_out_mla = jnp.pad(wo, ((0, 0), (0, LANES - MLA_V), (0, 0))).reshape(MLA_HEADS * LANES, D).astype(_BF16)
    return w_in_p, wq, wk, wv, w_out_na, w_out_mla


def _trunk(x, c_mods, P, layer_w):
    B, S, D = x.shape
    cos_h, sin_h = _head_rope_tables(S)
    cos_r, sin_r = _mla_rope_tables(S)
    for layer in range(DEPTH):
        j = layer // 2
        mm = c_mods[layer]
        lw = layer_w[layer]
        if layer % 2 == 0:
            proj = _inproj(x, mm["mix_shift"], mm["mix_scale"], lw["w_in"], cos_h, sin_h,
                           dil=1, rope_cols=0, half=1, tm=512).reshape(B, S, -1)
            na = _na_attention(proj, lw["bias_tab"])
            wlat = MLA_Q_RANK + MLA_KV_RANK + LANES
            q, k, v = _mla_prep(proj, (3 * NA_WIDTH) // wlat, lw["q_norm"], lw["kv_norm"],
                                lw["wq"], lw["wk"], lw["wv"], cos_r, sin_r, tm=512)
            mla = _mla_flash(q, k, v, tq=512, tk=2048)
            acts, weights, kind = [na, mla], [lw["w_out_na"], lw["w_out_mla"]], "ab"
        else:
            acts = []
            for g, (window, dil) in enumerate(DIL_PAIRS):
                assert (window // 2) // dil == DIL_HALF
                qkv = _inproj(x, mm["mix_shift"], mm["mix_scale"], lw["w_in"][g], cos_h, sin_h,
                              dil=dil, rope_cols=2 * DIL_WIDTH, half=HEAD_DIM // 2, tm=512)
                o, lse = _dil_attention(qkv)
                acts += [o, lse]
            weights, kind = [lw["w_out"]], "dil"
        x1, h2, aff_t = _outproj(kind, acts, weights, x, mm["mix_gate"], lw["mix_ln_g"], lw["mix_ln_b"],
                                 mm["moe_shift"], mm["moe_scale"], lw["wr_t"], tm=256)
        x = _expert_choice(x1, h2, aff_t, mm["moe_gate"], lw["moe_ln_g"], lw["moe_ln_b"],
                           lw["wg"], lw["wu"], lw["wd"], ffn_tm=512)
    return x


def kernel(x_prompt, x_sample, c_prompt, c_sample, ab_w_in, na_rpb, mla_q_norm, mla_w_q_up, mla_kv_norm,
           mla_w_kv_up, ab_w_out, dil_w_in, dil_w_out, mix_ada_w, mix_ada_b, mix_ln_g, mix_ln_b,
           moe_ada_w, moe_ada_b, moe_ln_g, moe_ln_b, moe_w_router, moe_w_gate, moe_w_up, moe_w_down):
    D = x_prompt.shape[-1]
    nb_p, nb_s = c_prompt.shape[0], c_sample.shape[0]
    bp = -(-(nb_p + nb_s) // 8) * 8
    c_all = jnp.zeros((bp, D), _F32).at[:nb_p].set(c_prompt).at[nb_p:nb_p + nb_s].set(c_sample)
    mix_mod = _ada_mod(c_all, mix_ada_w, mix_ada_b)
    moe_mod = _ada_mod(c_all, moe_ada_w, moe_ada_b)

    def mods_for(lo, n):
        out = []
        for layer in range(DEPTH):
            d = {}
            for name, mod in (("mix", mix_mod), ("moe", moe_mod)):
                m = mod[layer, lo:lo + n]
                d[name + "_shift"] = m[:, None, :D]
                d[name + "_scale"] = m[:, None, D:2 * D]
                d[name + "_gate"] = m[:, None, 2 * D:]
            out.append(d)
        return out

    layer_w = []
    for layer in range(DEPTH):
        j = layer // 2
        lw = {
            "mix_ln_g": mix_ln_g[layer][None], "mix_ln_b": mix_ln_b[layer][None],
            "moe_ln_g": moe_ln_g[layer][None], "moe_ln_b": moe_ln_b[layer][None],
            "wr_t": moe_w_router[layer].T,
            "wg": moe_w_gate[layer].astype(_BF16), "wu": moe_w_up[layer].astype(_BF16),
            "wd": moe_w_down[layer].astype(_BF16),
        }
        if layer % 2 == 0:
            w_in_p, wq, wk, wv, w_out_na, w_out_mla = _prep_ab_weights(
                ab_w_in[j], mla_w_q_up[j], mla_w_kv_up[j], ab_w_out[j])
            lw.update(w_in=w_in_p, wq=wq, wk=wk, wv=wv, w_out_na=w_out_na, w_out_mla=w_out_mla,
                      q_norm=mla_q_norm[j][None], kv_norm=mla_kv_norm[j][None],
                      bias_tab=_na_bias_table(na_rpb[j]))
        else:
            wi = dil_w_in[j].reshape(D, len(DIL_PAIRS), 3 * DIL_WIDTH).astype(_BF16)
            lw.update(w_in=[wi[:, g] for g in range(len(DIL_PAIRS))], w_out=dil_w_out[j].astype(_BF16))
        layer_w.append(lw)

    y_prompt = _trunk(x_prompt, mods_for(0, nb_p), None, layer_w)
    y_sample = _trunk(x_sample, mods_for(nb_p, nb_s), None, layer_w)
    return (y_prompt, y_sample)
```
